```python
import jax, jax.numpy as jnp
from jax import lax
import numpy as np

D_MODEL = 1024
BATCH = 2
SEQ = 8192
DEPTH = 2
DEC_BATCH = 32
DEC_SEQ = 16
PAST_LEN = 1024

CHUNK = 64
W_A = 512
CONV_A_WIDTH = 3
N_HEADS = 4
HEAD_K = 128
HEAD_V = 128
W_QK = N_HEADS * HEAD_K
W_VV = N_HEADS * HEAD_V
W_QKV = 2 * W_QK + W_VV
CONV_QKV_WIDTH = 4
D_FF = 4 * D_MODEL
IN_SIZES = (W_A, W_A, W_A, W_QKV, W_VV, N_HEADS, N_HEADS, D_MODEL, D_MODEL)
IN_WIDTH = 3 * W_A + W_QKV + W_VV + 2 * N_HEADS + 2 * D_MODEL
EPS = 1e-6

kernel_name = "hybrid_stream_gconv_gdn_step"


def _in_offsets():
    offs, acc = [], 0
    for s in IN_SIZES[:-1]:
        acc += s
        offs.append(acc)
    return offs


def rmsnorm(x, g):
    xf = x.astype(jnp.float32)
    y = xf * lax.rsqrt(jnp.mean(xf * xf, axis=-1, keepdims=True) + EPS)
    return (y * g.astype(jnp.float32)).astype(x.dtype)


def l2norm(x):
    xf = x.astype(jnp.float32)
    return xf * lax.rsqrt(jnp.sum(xf * xf, axis=-1, keepdims=True) + EPS)


def causal_conv(x, buf, w):
    width = w.shape[0]
    L = x.shape[1]
    xp = jnp.concatenate([buf.astype(x.dtype), x], axis=1)
    y = xp[:, 0:L] * w[0]
    for i in range(1, width):
        y = y + xp[:, i:i + L] * w[i]
    return y, xp[:, xp.shape[1] - (width - 1):]


def gated_delta_rule(q, k, v, beta, g, S0, chunk):
    Bn, L, H, K = q.shape
    V = v.shape[-1]
    N = L // chunk
    f32 = jnp.float32

    def blk(t):
        t = t.astype(f32).reshape((Bn, N, chunk, H) + t.shape[3:])
        return jnp.moveaxis(t, 3, 1)

    q, k, v, beta, g = blk(q), blk(k), blk(v), blk(beta), blk(g)
    gc = jnp.cumsum(g, axis=-1)
    incl = jnp.tril(jnp.ones((chunk, chunk), dtype=bool))
    strict = jnp.tril(jnp.ones((chunk, chunk), dtype=bool), k=-1)
    diff = gc[..., :, None] - gc[..., None, :]
    decay = jnp.exp(jnp.where(incl, diff, -jnp.inf))
    kk = jnp.einsum('bhnik,bhnjk->bhnij', k, k)
    A = jnp.where(strict, kk * decay * beta[..., :, None], 0.0)
    eye = jnp.eye(chunk, dtype=f32)
    rhs = jnp.concatenate([v * beta[..., None], k * (beta * jnp.exp(gc))[..., None]], axis=-1)
    sol = lax.linalg.triangular_solve(eye + A, rhs, left_side=True, lower=True,
                                      unit_diagonal=True)
    w_v, w_k = sol[..., :V], sol[..., V:]
    P = jnp.einsum('bhnik,bhnjk->bhnij', q, k) * decay
    q_g = q * jnp.exp(gc)[..., None]
    k_t = k * jnp.exp(gc[..., -1:] - gc)[..., None]
    g_last = jnp.exp(gc[..., -1])

    def step(S, inp):
        wv, wk, Pb, qg, kt, gl = inp
        U = wv - jnp.einsum('bhck,bhkv->bhcv', wk, S)
        o = jnp.einsum('bhck,bhkv->bhcv', qg, S) + jnp.einsum('bhcd,bhdv->bhcv', Pb, U)
        S = S * gl[..., None, None] + jnp.einsum('bhck,bhcv->bhkv', kt, U)
        return S, o

    xs = tuple(jnp.moveaxis(t, 2, 0) for t in (w_v, w_k, P, q_g, k_t, g_last))
    S_fin, o = lax.scan(step, S0.astype(f32), xs)
    o = jnp.transpose(o, (1, 0, 3, 2, 4)).reshape(Bn, L, H, V)
    return o, S_fin


def hybrid_layer(x, buf_a, buf_qkv, S0, chunk, norm1_g, w_in, conv_a_w, w_a_out,
                 conv_qkv_w, a_log, dt_bias, onorm_g, w_b_out, w_o, norm2_g, w_up, w_down):
    Bn, L, _ = x.shape
    xn = rmsnorm(x, norm1_g)
    proj = xn @ w_in
    b_a, c_a, x_a, qkv, z, beta_raw, a_raw, gate_a, gate_b = jnp.split(proj, _in_offsets(), axis=-1)

    u, new_buf_a = causal_conv(c_a * x_a, buf_a, conv_a_w)
    y_a = (b_a * u) @ w_a_out

    qkv_c, new_buf_qkv = causal_conv(qkv, buf_qkv, conv_qkv_w)
    qkv_c = jax.nn.silu(qkv_c)
    q, k, v = jnp.split(qkv_c, [W_QK, 2 * W_QK], axis=-1)
    q = l2norm(q.reshape(Bn, L, N_HEADS, HEAD_K)) * (HEAD_K ** -0.5)
    k = l2norm(k.reshape(Bn, L, N_HEADS, HEAD_K))
    v = v.reshape(Bn, L, N_HEADS, HEAD_V).astype(jnp.float32)
    beta = jax.nn.sigmoid(beta_raw.astype(jnp.float32))
    g = -jnp.exp(a_log.astype(jnp.float32)) * jax.nn.softplus(
        a_raw.astype(jnp.float32) + dt_bias.astype(jnp.float32))
    o, S_new = gated_delta_rule(q, k, v, beta, g, S0, chunk)
    o = rmsnorm(o, onorm_g) * jax.nn.silu(z.reshape(Bn, L, N_HEADS, HEAD_V).astype(jnp.float32))
    y_b = o.reshape(Bn, L, W_VV).astype(x.dtype) @ w_b_out

    mixed = jax.nn.sigmoid(gate_a) * y_a + jax.nn.sigmoid(gate_b) * y_b
    h = x + mixed @ w_o
    f = jnp.square(jax.nn.relu(rmsnorm(h, norm2_g) @ w_up)) @ w_down
    return h + f, new_buf_a, new_buf_qkv, S_new


def run_trunk(x, bufs_a, bufs_qkv, Ss, chunk, norm1_g, w_in, conv_a_w, w_a_out,
              conv_qkv_w, a_log, dt_bias, onorm_g, w_b_out, w_o, norm2_g, w_up, w_down,
              final_g):
    na, nq, ns = [], [], []
    for l in range(DEPTH):
        x, ba, bq, S = hybrid_layer(x, bufs_a[l], bufs_qkv[l], Ss[l], chunk, norm1_g[l], w_in[l],
                                    conv_a_w[l], w_a_out[l], conv_qkv_w[l], a_log[l], dt_bias[l],
                                    onorm_g[l], w_b_out[l], w_o[l], norm2_g[l], w_up[l], w_down[l])
        na.append(ba)
        nq.append(bq)
        ns.append(S)
    return rmsnorm(x, final_g), jnp.stack(na), jnp.stack(nq), jnp.stack(ns)


def setup_inputs(seed: int = 0) -> dict:
    key = jax.random.key(seed)
    ks = jax.random.split(key, 24)
    f32 = jnp.float32
    nrm = lambda k, s, sc: jax.random.normal(k, s, f32) * sc
    dt = jnp.exp(jax.random.uniform(ks[10], (DEPTH, N_HEADS), f32, np.log(1e-3), np.log(1e-1)))
    return {
        "x_prompt": nrm(ks[0], (BATCH, SEQ, D_MODEL), 1.0),
        "x_sample": nrm(ks[1], (DEC_BATCH, DEC_SEQ, D_MODEL), 1.0),
        "state_conv_a": nrm(ks[2], (DEPTH, DEC_BATCH, CONV_A_WIDTH - 1, W_A), 1.0),
        "state_conv_qkv": nrm(ks[3], (DEPTH, DEC_BATCH, CONV_QKV_WIDTH - 1, W_QKV), 1.0),
        "state_delta": nrm(ks[4], (DEPTH, DEC_BATCH, N_HEADS, HEAD_K, HEAD_V), HEAD_K ** -0.5),
        "norm1_g": 1.0 + nrm(ks[5], (DEPTH, D_MODEL), 0.02),
        "w_in": nrm(ks[6], (DEPTH, D_MODEL, IN_WIDTH), D_MODEL ** -0.5),
        "conv_a_w": nrm(ks[7], (DEPTH, CONV_A_WIDTH, W_A), CONV_A_WIDTH ** -0.5),
        "w_a_out": nrm(ks[8], (DEPTH, W_A, D_MODEL), W_A ** -0.5),
        "conv_qkv_w": nrm(ks[9], (DEPTH, CONV_QKV_WIDTH, W_QKV), CONV_QKV_WIDTH ** -0.5),
        "a_log": jnp.log(jax.random.uniform(ks[11], (DEPTH, N_HEADS), f32, 1.0, 16.0)),
        "dt_bias": dt + jnp.log(-jnp.expm1(-dt)),
        "onorm_g": 1.0 + nrm(ks[12], (DEPTH, HEAD_V), 0.02),
        "w_b_out": nrm(ks[13], (DEPTH, W_VV, D_MODEL), W_VV ** -0.5),
        "w_o": nrm(ks[14], (DEPTH, D_MODEL, D_MODEL), D_MODEL ** -0.5),
        "norm2_g": 1.0 + nrm(ks[15], (DEPTH, D_MODEL), 0.02),
        "w_up": nrm(ks[16], (DEPTH, D_MODEL, D_FF), D_MODEL ** -0.5),
        "w_down": nrm(ks[17], (DEPTH, D_FF, D_MODEL), D_FF ** -0.5),
        "final_g": 1.0 + nrm(ks[18], (D_MODEL,), 0.02),
    }


def reference(x_prompt, x_sample, state_conv_a, state_conv_qkv, state_delta, norm1_g, w_in,
              conv_a_w, w_a_out, conv_qkv_w, a_log, dt_bias, onorm_g, w_b_out, w_o, norm2_g,
              w_up, w_down, final_g):
    nb = x_prompt.shape[0]
    zero_a = jnp.zeros((DEPTH, nb, CONV_A_WIDTH - 1, W_A), x_prompt.dtype)
    zero_qkv = jnp.zeros((DEPTH, nb, CONV_QKV_WIDTH - 1, W_QKV), x_prompt.dtype)
    zero_S = jnp.zeros((DEPTH, nb, N_HEADS, HEAD_K, HEAD_V), jnp.float32)
    y_prompt, p_a, p_qkv, p_S = run_trunk(
        x_prompt, zero_a, zero_qkv, zero_S, CHUNK, norm1_g, w_in, conv_a_w, w_a_out,
        conv_qkv_w, a_log, dt_bias, onorm_g, w_b_out, w_o, norm2_g, w_up, w_down, final_g)
    y_sample, s_a, s_qkv, s_S = run_trunk(
        x_sample, state_conv_a, state_conv_qkv, state_delta, x_sample.shape[1], norm1_g, w_in,
        conv_a_w, w_a_out, conv_qkv_w, a_log, dt_bias, onorm_g, w_b_out, w_o, norm2_g, w_up,
        w_down, final_g)
    return (y_prompt, y_sample,
            p_a.astype(state_conv_a.dtype), p_qkv.astype(state_conv_qkv.dtype),
            p_S.astype(state_delta.dtype),
            s_a.astype(state_conv_a.dtype), s_qkv.astype(state_conv_qkv.dtype),
            s_S.astype(state_delta.dtype))
```

```python
import functools

import jax
import jax.numpy as jnp
from jax import lax
from jax.experimental import pallas as pl
from jax.experimental.pallas import tpu as pltpu

F32 = jnp.float32
BF16 = jnp.bfloat16

D_MODEL = 1024
W_A = 512
N_HEADS = 4
HEAD_K = 128
HEAD_V = 128
W_QK = N_HEADS * HEAD_K
W_VV = N_HEADS * HEAD_V
W_QKV = 2 * W_QK + W_VV
CONV_A_WIDTH = 3
CONV_QKV_WIDTH = 4
D_FF = 4 * D_MODEL
EPS = 1e-6

LANES = 128
SUBLANES = 8
BLOCK_ROWS = 128

OFF_A = 0
OFF_QKV = OFF_A + 3 * W_A
OFF_Z = OFF_QKV + W_QKV
OFF_GA = OFF_Z + W_VV
OFF_GB = OFF_GA + D_MODEL
OFF_BETA = OFF_GB + D_MODEL
OFF_ALPHA = OFF_BETA + LANES
IN_PACKED = OFF_ALPHA + LANES

MIXER_TILE = 256
MLP_TILE = 512
FF_CHUNK = 1024
VMEM_LIMIT = 56 * 1024 * 1024


def _mm(a, b):
    return jnp.dot(a.astype(BF16), b.astype(BF16), preferred_element_type=F32)


def _mm_nt(a, b):
    return lax.dot_general(a.astype(BF16), b.astype(BF16), (((1,), (1,)), ((), ())),
                           preferred_element_type=F32)


def _mm_tn(a, b):
    return lax.dot_general(a.astype(BF16), b.astype(BF16), (((0,), (0,)), ((), ())),
                           preferred_element_type=F32)


def _rms(x, g):
    return x * lax.rsqrt(jnp.mean(x * x, axis=-1, keepdims=True) + EPS) * g


def _sigmoid(x):
    return 1.0 / (1.0 + jnp.exp(-x))


def _softplus(x):
    return jnp.maximum(x, 0.0) + jnp.log1p(jnp.exp(-jnp.abs(x)))


def _chunk_scan(x, pos, chunk, reverse):
    rows = x.shape[0]
    s = 1
    while s < chunk:
        if reverse:
            x = x + jnp.where(pos < chunk - s, pltpu.roll(x, rows - s, 0), 0.0)
        else:
            x = x + jnp.where(pos >= s, pltpu.roll(x, s, 0), 0.0)
        s *= 2
    return x


def _mixer_kernel(x_ref, g1_ref, win_ref, caw_ref, waout_ref, cqw_ref, alog_ref, dtb_ref, og_ref,
                  wbout_ref, wo_ref, sa_ref, sq_ref, s0_ref,
                  h_ref, na_ref, nq_ref, ns_ref,
                  pad_a, pad_q, o_scr, *, tm, chunk, streaming):
    nseq = tm // chunk
    row = lax.broadcasted_iota(jnp.int32, (tm, LANES), 0)
    pos = row & (chunk - 1)

    if streaming:
        first = pl.program_id(1) == 0

        @pl.when(first)
        def _():
            ns_ref[...] = s0_ref[...]

    def causal_conv(pad, xin, w_ref, width, state_ref, new_ref):
        hw = width - 1
        if streaming:
            @pl.when(first)
            def _():
                pad[SUBLANES - hw:SUBLANES, :] = state_ref[...]
            pad[SUBLANES:SUBLANES + tm, :] = xin
            y = xin * w_ref[hw:hw + 1, :]
            for s in range(1, width):
                y = y + pad[SUBLANES - s:SUBLANES - s + tm, :] * w_ref[hw - s:hw - s + 1, :]
            new_ref[...] = pad[SUBLANES + tm - hw:SUBLANES + tm, :]
            pad[0:SUBLANES, :] = pad[tm:tm + SUBLANES, :]
            return y
        pad[0:tm, :] = xin
        pad[tm:2 * tm, :] = jnp.zeros((tm, xin.shape[1]), F32)
        for q in range(nseq):
            qp = (q - 1) % nseq
            pad[tm + qp * chunk + chunk - hw:tm + qp * chunk + chunk, :] = state_ref[q]
            new_ref[q] = pad[q * chunk + chunk - hw:q * chunk + chunk, :]
        hist = pad[tm:2 * tm, :]
        cpos = lax.broadcasted_iota(jnp.int32, (tm, 1), 0) & (chunk - 1)
        y = xin * w_ref[hw:hw + 1, :]
        for s in range(1, width):
            sh = jnp.where(cpos >= s, pltpu.roll(xin, s, 0), pltpu.roll(hist, s, 0))
            y = y + sh * w_ref[hw - s:hw - s + 1, :]
        return y

    x = x_ref[...]
    xn = _rms(x, g1_ref[...]).astype(BF16)

    p_a = _mm(xn, win_ref[:, OFF_A:OFF_A + 3 * W_A])
    u = causal_conv(pad_a, p_a[:, W_A:2 * W_A] * p_a[:, 2 * W_A:3 * W_A], caw_ref, CONV_A_WIDTH,
                    sa_ref, na_ref)
    y_a = _mm(p_a[:, 0:W_A] * u, waout_ref[...])

    p_q = _mm(xn, win_ref[:, OFF_QKV:OFF_QKV + W_QKV])
    c = causal_conv(pad_q, p_q, cqw_ref, CONV_QKV_WIDTH, sq_ref, nq_ref)
    c = c * _sigmoid(c)

    beta = _sigmoid(_mm(xn, win_ref[:, OFF_BETA:OFF_BETA + LANES]))
    a_raw = _mm(xn, win_ref[:, OFF_ALPHA:OFF_ALPHA + LANES])
    g = -jnp.exp(alog_ref[...]) * _softplus(a_raw + dtb_ref[...])
    gc = _chunk_scan(g, pos, chunk, reverse=False)
    g_after = _chunk_scan(g, pos, chunk, reverse=True) - g
    eg = jnp.exp(gc)
    e_after = jnp.exp(g_after)
    gc_t = gc.T

    ri = lax.broadcasted_iota(jnp.int32, (BLOCK_ROWS, BLOCK_ROWS), 0)
    ci = lax.broadcasted_iota(jnp.int32, (BLOCK_ROWS, BLOCK_ROWS), 1)
    m_incl = ((ri & -chunk) == (ci & -chunk)) & (ri >= ci)
    eye = jnp.where(ri == ci, 1.0, 0.0)
    m_levels = []
    size = 1
    while size < chunk:
        m_levels.append(((ri & -(2 * size)) == (ci & -(2 * size))) & ((ri & -size) > (ci & -size)))
        size *= 2

    for rb in range(tm // BLOCK_ROWS):
        r0 = rb * BLOCK_ROWS
        rows = slice(r0, r0 + BLOCK_ROWS)
        for h in range(N_HEADS):
            qh = c[rows, h * HEAD_K:(h + 1) * HEAD_K]
            kh = c[rows, W_QK + h * HEAD_K:W_QK + (h + 1) * HEAD_K]
            vh = c[rows, 2 * W_QK + h * HEAD_V:2 * W_QK + (h + 1) * HEAD_V]
            qh = qh * (lax.rsqrt(jnp.sum(qh * qh, axis=-1, keepdims=True) + EPS) * (HEAD_K ** -0.5))
            kh = kh * lax.rsqrt(jnp.sum(kh * kh, axis=-1, keepdims=True) + EPS)
            b_col = beta[rows, h:h + 1]
            gc_col = gc[rows, h:h + 1]
            eg_col = eg[rows, h:h + 1]
            ea_col = e_after[rows, h:h + 1]
            gc_row = gc_t[h:h + 1, rows]

            diff = gc_col - gc_row
            dec_incl = jnp.exp(jnp.where(m_incl, diff, -1e30))
            qk = _mm_nt(jnp.concatenate([qh, kh], axis=0), kh)
            p_blk = qk[0:BLOCK_ROWS] * dec_incl
            a_blk = qk[BLOCK_ROWS:] * dec_incl * b_col

            t_inv = eye - jnp.where(m_levels[0], a_blk, 0.0)
            for m_off in m_levels[1:]:
                t_inv = t_inv - _mm(t_inv, _mm(jnp.where(m_off, a_blk, 0.0), t_inv))
            rhs = jnp.concatenate([vh * b_col, kh * (b_col * eg_col)], axis=1)
            w_vk = _mm(t_inv, rhs)
            w_v = w_vk[:, 0:HEAD_V]
            w_k = w_vk[:, HEAD_V:]
            q_g = qh * eg_col
            k_t = kh * ea_col

            for cc in range(BLOCK_ROWS // chunk):
                c0 = cc * chunk
                q_idx = rb * (BLOCK_ROWS // chunk) + cc
                if streaming:
                    s_old = ns_ref[h]
                else:
                    s_old = s0_ref[q_idx, h]
                xs = _mm(jnp.concatenate([w_k[c0:c0 + chunk], q_g[c0:c0 + chunk]], axis=0), s_old)
                u_c = w_v[c0:c0 + chunk] - xs[0:chunk]
                o_c = xs[chunk:] + _mm(p_blk[c0:c0 + chunk, c0:c0 + chunk], u_c)
                g_last = eg[r0 + c0 + chunk - 1:r0 + c0 + chunk, h:h + 1]
                s_new = s_old * g_last + _mm_tn(k_t[c0:c0 + chunk], u_c)
                if streaming:
                    ns_ref[h] = s_new
                else:
                    ns_ref[q_idx, h] = s_new
                o_scr[r0 + c0:r0 + c0 + chunk, h * HEAD_V:(h + 1) * HEAD_V] = o_c

    z = _mm(xn, win_ref[:, OFF_Z:OFF_Z + W_VV])
    og = og_ref[...]
    o_parts = []
    for h in range(N_HEADS):
        o_h = o_scr[:, h * HEAD_V:(h + 1) * HEAD_V]
        z_h = z[:, h * HEAD_V:(h + 1) * HEAD_V]
        o_parts.append(_rms(o_h, og) * (z_h * _sigmoid(z_h)))
    y_b = _mm(jnp.concatenate(o_parts, axis=1), wbout_ref[...])

    gate_a = _mm(xn, win_ref[:, OFF_GA:OFF_GA + D_MODEL])
    gate_b = _mm(xn, win_ref[:, OFF_GB:OFF_GB + D_MODEL])
    mixed = _sigmoid(gate_a) * y_a + _sigmoid(gate_b) * y_b
    h_ref[...] = x + _mm(mixed, wo_ref[...])


def _mlp_kernel(h_ref, g2_ref, wup_ref, wdown_ref, gf_ref, out_ref, *, final_norm):
    h = h_ref[...]
    hn = _rms(h, g2_ref[...]).astype(BF16)
    acc = h
    for j in range(D_FF // FF_CHUNK):
        up = jnp.dot(hn, wup_ref[:, j * FF_CHUNK:(j + 1) * FF_CHUNK], preferred_element_type=F32)
        act = jnp.maximum(up, 0.0)
        acc = acc + _mm(act * act, wdown_ref[j * FF_CHUNK:(j + 1) * FF_CHUNK, :])
    if final_norm:
        acc = _rms(acc, gf_ref[...])
    out_ref[...] = acc


def _resident(shape):
    nd = len(shape)
    return pl.BlockSpec(shape, lambda *_: (0,) * nd, pipeline_mode=pl.Buffered(1))


def _mixer_weight_specs():
    return [
        _resident((1, D_MODEL)),
        _resident((D_MODEL, IN_PACKED)),
        _resident((CONV_A_WIDTH, W_A)),
        _resident((W_A, D_MODEL)),
        _resident((CONV_QKV_WIDTH, W_QKV)),
        _resident((1, LANES)),
        _resident((1, LANES)),
        _resident((1, HEAD_V)),
        _resident((W_VV, D_MODEL)),
        _resident((D_MODEL, D_MODEL)),
    ]


def _mixer_stream(x, buf_a, buf_qkv, s0, weights):
    nb, seq, _ = x.shape
    tm = MIXER_TILE
    chunk = 64
    kern = functools.partial(_mixer_kernel, tm=tm, chunk=chunk, streaming=True)
    per_b3 = lambda b, i: (b, 0, 0)
    per_b4 = lambda b, i: (b, 0, 0, 0)
    return pl.pallas_call(
        kern,
        grid=(nb, seq // tm),
        in_specs=[pl.BlockSpec((None, tm, D_MODEL), lambda b, i: (b, i, 0))] + _mixer_weight_specs() + [
            pl.BlockSpec((None, CONV_A_WIDTH - 1, W_A), per_b3),
            pl.BlockSpec((None, CONV_QKV_WIDTH - 1, W_QKV), per_b3),
            pl.BlockSpec((None, N_HEADS, HEAD_K, HEAD_V), per_b4),
        ],
        out_specs=[
            pl.BlockSpec((None, tm, D_MODEL), lambda b, i: (b, i, 0)),
            pl.BlockSpec((None, CONV_A_WIDTH - 1, W_A), per_b3),
            pl.BlockSpec((None, CONV_QKV_WIDTH - 1, W_QKV), per_b3),
            pl.BlockSpec((None, N_HEADS, HEAD_K, HEAD_V), per_b4),
        ],
        out_shape=[
            jax.ShapeDtypeStruct(x.shape, F32),
            jax.ShapeDtypeStruct(buf_a.shape, F32),
            jax.ShapeDtypeStruct(buf_qkv.shape, F32),
            jax.ShapeDtypeStruct(s0.shape, F32),
        ],
        scratch_shapes=[
            pltpu.VMEM((SUBLANES + tm, W_A), F32),
            pltpu.VMEM((SUBLANES + tm, W_QKV), F32),
            pltpu.VMEM((tm, W_VV), F32),
        ],
        compiler_params=pltpu.CompilerParams(
            dimension_semantics=("arbitrary", "arbitrary"), vmem_limit_bytes=VMEM_LIMIT),
        name="mixer_stream",
    )(x, *weights, buf_a, buf_qkv, s0)


def _mixer_step(x, buf_a, buf_qkv, s0, weights):
    nb, seq, _ = x.shape
    tm = BLOCK_ROWS
    nseq = tm // seq
    kern = functools.partial(_mixer_kernel, tm=tm, chunk=seq, streaming=False)
    grp3 = lambda i: (i, 0, 0)
    grp4 = lambda i: (i, 0, 0, 0)
    x2 = x.reshape(nb * seq, D_MODEL)
    h2, na, nq, ns = pl.pallas_call(
        kern,
        grid=(nb // nseq,),
        in_specs=[pl.BlockSpec((tm, D_MODEL), lambda i: (i, 0))] + _mixer_weight_specs() + [
            pl.BlockSpec((nseq, CONV_A_WIDTH - 1, W_A), grp3),
            pl.BlockSpec((nseq, CONV_QKV_WIDTH - 1, W_QKV), grp3),
            pl.BlockSpec((nseq, N_HEADS, HEAD_K, HEAD_V), grp4),
        ],
        out_specs=[
            pl.BlockSpec((tm, D_MODEL), lambda i: (i, 0)),
            pl.BlockSpec((nseq, CONV_A_WIDTH - 1, W_A), grp3),
            pl.BlockSpec((nseq, CONV_QKV_WIDTH - 1, W_QKV), grp3),
            pl.BlockSpec((nseq, N_HEADS, HEAD_K, HEAD_V), grp4),
        ],
        out_shape=[
            jax.ShapeDtypeStruct(x2.shape, F32),
            jax.ShapeDtypeStruct(buf_a.shape, F32),
            jax.ShapeDtypeStruct(buf_qkv.shape, F32),
            jax.ShapeDtypeStruct(s0.shape, F32),
        ],
        scratch_shapes=[
            pltpu.VMEM((2 * tm, W_A), F32),
            pltpu.VMEM((2 * tm, W_QKV), F32),
            pltpu.VMEM((tm, W_VV), F32),
        ],
        compiler_params=pltpu.CompilerParams(
            dimension_semantics=("arbitrary",), vmem_limit_bytes=VMEM_LIMIT),
        name="mixer_step",
    )(x2, *weights, buf_a, buf_qkv, s0)
    return h2.reshape(x.shape), na, nq, ns


def _mlp(h, g2, w_up, w_down, gf, final_norm):
    shape = h.shape
    h2 = h.reshape(-1, D_MODEL)
    rows = h2.shape[0]
    tm = min(MLP_TILE, rows)
    out = pl.pallas_call(
        functools.partial(_mlp_kernel, final_norm=final_norm),
        grid=(rows // tm,),
        in_specs=[
            pl.BlockSpec((tm, D_MODEL), lambda i: (i, 0)),
            _resident((1, D_MODEL)),
            _resident((D_MODEL, D_FF)),
            _resident((D_FF, D_MODEL)),
            _resident((1, D_MODEL)),
        ],
        out_specs=pl.BlockSpec((tm, D_MODEL), lambda i: (i, 0)),
        out_shape=jax.ShapeDtypeStruct(h2.shape, F32),
        compiler_params=pltpu.CompilerParams(
            dimension_semantics=("arbitrary",), vmem_limit_bytes=VMEM_LIMIT),
        name="mlp",
    )(h2, g2, w_up, w_down, gf)
    return out.reshape(shape)


def _pack_w_in(w):
    n_main = 3 * W_A + W_QKV + W_VV
    main = w[:, :n_main]
    beta = w[:, n_main:n_main + N_HEADS]
    alpha = w[:, n_main + N_HEADS:n_main + 2 * N_HEADS]
    gates = w[:, n_main + 2 * N_HEADS:]
    zpad = jnp.zeros((w.shape[0], LANES - N_HEADS), w.dtype)
    return jnp.concatenate([main, gates, beta, zpad, alpha, zpad], axis=1).astype(BF16)


def _head_row(v):
    return jnp.pad(v.astype(F32), (0, LANES - N_HEADS)).reshape(1, LANES)


def kernel(x_prompt, x_sample, state_conv_a, state_conv_qkv, state_delta, norm1_g, w_in, conv_a_w,
           w_a_out, conv_qkv_w, a_log, dt_bias, onorm_g, w_b_out, w_o, norm2_g, w_up, w_down,
           final_g):
    depth = w_in.shape[0]
    nb = x_prompt.shape[0]
    zero_a = jnp.zeros((nb, CONV_A_WIDTH - 1, W_A), F32)
    zero_qkv = jnp.zeros((nb, CONV_QKV_WIDTH - 1, W_QKV), F32)
    zero_s = jnp.zeros((nb, N_HEADS, HEAD_K, HEAD_V), F32)
    gf = final_g.reshape(1, D_MODEL)

    xp, xs = x_prompt, x_sample
    outs_p = ([], [], [])
    outs_s = ([], [], [])
    for l in range(depth):
        weights = (
            norm1_g[l].reshape(1, D_MODEL), _pack_w_in(w_in[l]), conv_a_w[l], w_a_out[l].astype(BF16),
            conv_qkv_w[l], _head_row(a_log[l]), _head_row(dt_bias[l]), onorm_g[l].reshape(1, HEAD_V),
            w_b_out[l].astype(BF16), w_o[l].astype(BF16))
        g2 = norm2_g[l].reshape(1, D_MODEL)
        wu = w_up[l].astype(BF16)
        wd = w_down[l].astype(BF16)
        last = l == depth - 1

        hp, pa, pq, ps = _mixer_stream(xp, zero_a, zero_qkv, zero_s, weights)
        xp = _mlp(hp, g2, wu, wd, gf, last)
        hs, sa, sq, ss = _mixer_step(xs, state_conv_a[l], state_conv_qkv[l], state_delta[l], weights)
        xs = _mlp(hs, g2, wu, wd, gf, last)
        for acc, val in zip(outs_p, (pa, pq, ps)):
            acc.append(val)
        for acc, val in zip(outs_s, (sa, sq, ss)):
            acc.append(val)

    return (xp, xs,
            jnp.stack(outs_p[0]), jnp.stack(outs_p[1]), jnp.stack(outs_p[2]),
            jnp.stack(outs_s[0]), jnp.stack(outs_s[1]), jnp.stack(outs_s[2]))
```

```python
import functools

import jax
import jax.numpy as jnp
from jax import lax
from jax.experimental import pallas as pl
from jax.experimental.pallas import tpu as pltpu

F32 = jnp.float32
BF16 = jnp.bfloat16

D_MODEL = 1024
W_A = 512
N_HEADS = 4
HEAD_K = 128
HEAD_V = 128
W_QK = N_HEADS * HEAD_K
W_VV = N_HEADS * HEAD_V
W_QKV = 2 * W_QK + W_VV
CONV_A_WIDTH = 3
CONV_QKV_WIDTH = 4
D_FF = 4 * D_MODEL
EPS = 1e-6

LANES = 128
SUBLANES = 8
BLOCK_ROWS = 128

OFF_A = 0
OFF_QKV = OFF_A + 3 * W_A
OFF_Z = OFF_QKV + W_QKV
OFF_GA = OFF_Z + W_VV
OFF_GB = OFF_GA + D_MODEL
OFF_BETA = OFF_GB + D_MODEL
OFF_ALPHA = OFF_BETA + LANES
IN_PACKED = OFF_ALPHA + LANES

MIXER_TILE = 256
MLP_TILE = 512
FF_CHUNK = 1024
VMEM_LIMIT = 56 * 1024 * 1024


def _mm(a, b):
    return jnp.dot(a.astype(BF16), b.astype(BF16), preferred_element_type=F32)


def _mm_nt(a, b):
    return lax.dot_general(a.astype(BF16), b.astype(BF16), (((1,), (1,)), ((), ())),
                           preferred_element_type=F32)


def _mm_tn(a, b):
    return lax.dot_general(a.astype(BF16), b.astype(BF16), (((0,), (0,)), ((), ())),
                           preferred_element_type=F32)


def _rms(x, g):
    return x * lax.rsqrt(jnp.mean(x * x, axis=-1, keepdims=True) + EPS) * g


def _sigmoid(x):
    return 1.0 / (1.0 + jnp.exp(-x))


def _softplus(x):
    return jnp.maximum(x, 0.0) + jnp.log1p(jnp.exp(-jnp.abs(x)))


def _chunk_scan(x, pos, chunk, reverse):
    rows = x.shape[0]
    s = 1
    while s < chunk:
        if reverse:
            x = x + jnp.where(pos < chunk - s, pltpu.roll(x, rows - s, 0), 0.0)
        else:
            x = x + jnp.where(pos >= s, pltpu.roll(x, s, 0), 0.0)
        s *= 2
    return x


def _mixer_kernel(x_ref, g1_ref, win_ref, caw_ref, waout_ref, cqw_ref, alog_ref, dtb_ref, og_ref,
                  wbout_ref, wo_ref, sa_ref, sq_ref, s0_ref,
                  h_ref, na_ref, nq_ref, ns_ref,
                  pad_a, pad_q, o_scr, *, tm, chunk, streaming):
    nseq = tm // chunk
    row = lax.broadcasted_iota(jnp.int32, (tm, LANES), 0)
    pos = row & (chunk - 1)

    if streaming:
        first = pl.program_id(1) == 0

        @pl.when(first)
        def _():
            ns_ref[...] = s0_ref[...]

    def causal_conv(pad, xin, w_ref, width, state_ref, new_ref):
        hw = width - 1
        if streaming:
            @pl.when(first)
            def _():
                pad[SUBLANES - hw:SUBLANES, :] = state_ref[...]
            pad[SUBLANES:SUBLANES + tm, :] = xin
            y = xin * w_ref[hw:hw + 1, :]
            for s in range(1, width):
                y = y + pad[SUBLANES - s:SUBLANES - s + tm, :] * w_ref[hw - s:hw - s + 1, :]
            new_ref[...] = pad[SUBLANES + tm - hw:SUBLANES + tm, :]
            pad[0:SUBLANES, :] = pad[tm:tm + SUBLANES, :]
            return y
        pad[0:tm, :] = xin
        pad[tm:2 * tm, :] = jnp.zeros((tm, xin.shape[1]), F32)
        for q in range(nseq):
            qp = (q - 1) % nseq
            pad[tm + qp * chunk + chunk - hw:tm + qp * chunk + chunk, :] = state_ref[q]
            new_ref[q] = pad[q * chunk + chunk - hw:q * chunk + chunk, :]
        hist = pad[tm:2 * tm, :]
        cpos = lax.broadcasted_iota(jnp.int32, (tm, 1), 0) & (chunk - 1)
        y = xin * w_ref[hw:hw + 1, :]
        for s in range(1, width):
            sh = jnp.where(cpos >= s, pltpu.roll(xin, s, 0), pltpu.roll(hist, s, 0))
            y = y + sh * w_ref[hw - s:hw - s + 1, :]
        return y

    x = x_ref[...]
    xn = _rms(x, g1_ref[...]).astype(BF16)

    p_a = _mm(xn, win_ref[:, OFF_A:OFF_A + 3 * W_A])
    u = causal_conv(pad_a, p_a[:, W_A:2 * W_A] * p_a[:, 2 * W_A:3 * W_A], caw_ref, CONV_A_WIDTH,
                    sa_ref, na_ref)
    y_a = _mm(p_a[:, 0:W_A] * u, waout_ref[...])

    p_q = _mm(xn, win_ref[:, OFF_QKV:OFF_QKV + W_QKV])
    c = causal_conv(pad_q, p_q, cqw_ref, CONV_QKV_WIDTH, sq_ref, nq_ref)
    c = c * _sigmoid(c)

    beta = _sigmoid(_mm(xn, win_ref[:, OFF_BETA:OFF_BETA + LANES]))
    a_raw = _mm(xn, win_ref[:, OFF_ALPHA:OFF_ALPHA + LANES])
    g = -jnp.exp(alog_ref[...]) * _softplus(a_raw + dtb_ref[...])
    gc = _chunk_scan(g, pos, chunk, reverse=False)
    g_after = _chunk_scan(g, pos, chunk, reverse=True) - g
    eg = jnp.exp(gc)
    e_after = jnp.exp(g_after)
    gc_t = gc.T

    ri = lax.broadcasted_iota(jnp.int32, (BLOCK_ROWS, BLOCK_ROWS), 0)
    ci = lax.broadcasted_iota(jnp.int32, (BLOCK_ROWS, BLOCK_ROWS), 1)
    m_incl = ((ri & -chunk) == (ci & -chunk)) & (ri >= ci)
    eye = jnp.where(ri == ci, 1.0, 0.0)
    m_levels = []
    size = 1
    while size < chunk:
        m_levels.append(((ri & -(2 * size)) == (ci & -(2 * size))) & ((ri & -size) > (ci & -size)))
        size *= 2

    pairs = [(rb, h) for rb in range(tm // BLOCK_ROWS) for h in range(N_HEADS)]
    q_g, k_t, rhs, p_blk, a_blk, t_inv = {}, {}, {}, {}, {}, {}
    for rb, h in pairs:
        rows = slice(rb * BLOCK_ROWS, (rb + 1) * BLOCK_ROWS)
        qh = c[rows, h * HEAD_K:(h + 1) * HEAD_K]
        kh = c[rows, W_QK + h * HEAD_K:W_QK + (h + 1) * HEAD_K]
        vh = c[rows, 2 * W_QK + h * HEAD_V:2 * W_QK + (h + 1) * HEAD_V]
        qh = qh * (lax.rsqrt(jnp.sum(qh * qh, axis=-1, keepdims=True) + EPS) * (HEAD_K ** -0.5))
        kh = kh * lax.rsqrt(jnp.sum(kh * kh, axis=-1, keepdims=True) + EPS)
        b_col = beta[rows, h:h + 1]
        eg_col = eg[rows, h:h + 1]
        diff = gc[rows, h:h + 1] - gc_t[h:h + 1, rows]
        dec_incl = jnp.exp(jnp.where(m_incl, diff, -1e30))
        qk = _mm_nt(jnp.concatenate([qh, kh], axis=0), kh)
        p_blk[rb, h] = qk[0:BLOCK_ROWS] * dec_incl
        a_blk[rb, h] = qk[BLOCK_ROWS:] * dec_incl * b_col
        rhs[rb, h] = jnp.concatenate([vh * b_col, kh * (b_col * eg_col)], axis=1)
        q_g[rb, h] = qh * eg_col
        k_t[rb, h] = kh * e_after[rows, h:h + 1]
        t_inv[rb, h] = eye - jnp.where(m_levels[0], a_blk[rb, h], 0.0)

    for m_off in m_levels[1:]:
        a_t = {p: _mm(jnp.where(m_off, a_blk[p], 0.0), t_inv[p]) for p in pairs}
        t_inv = {p: t_inv[p] - _mm(t_inv[p], a_t[p]) for p in pairs}
    w_vk = {p: _mm(t_inv[p], rhs[p]) for p in pairs}

    cpb = BLOCK_ROWS // chunk
    steps = [[(q, h) for h in range(N_HEADS)] for q in range(tm // chunk)] if streaming else \
        [[(q, h) for q in range(tm // chunk) for h in range(N_HEADS)]]
    for step in steps:
        s_old, xs, u_c = {}, {}, {}
        for q, h in step:
            p, c0 = (q // cpb, h), (q % cpb) * chunk
            s_old[q, h] = ns_ref[h] if streaming else s0_ref[q, h]
            xs[q, h] = _mm(jnp.concatenate([w_vk[p][c0:c0 + chunk, HEAD_V:], q_g[p][c0:c0 + chunk]],
                                           axis=0), s_old[q, h])
        for q, h in step:
            p, c0 = (q // cpb, h), (q % cpb) * chunk
            u_c[q, h] = w_vk[p][c0:c0 + chunk, 0:HEAD_V] - xs[q, h][0:chunk]
            o_c = xs[q, h][chunk:] + _mm(p_blk[p][c0:c0 + chunk, c0:c0 + chunk], u_c[q, h])
            o_scr[q * chunk:(q + 1) * chunk, h * HEAD_V:(h + 1) * HEAD_V] = o_c
        for q, h in step:
            p, c0 = (q // cpb, h), (q % cpb) * chunk
            g_last = eg[(q + 1) * chunk - 1:(q + 1) * chunk, h:h + 1]
            s_new = s_old[q, h] * g_last + _mm_tn(k_t[p][c0:c0 + chunk], u_c[q, h])
            if streaming:
                ns_ref[h] = s_new
            else:
                ns_ref[q, h] = s_new


    z = _mm(xn, win_ref[:, OFF_Z:OFF_Z + W_VV])
    og = og_ref[...]
    o_parts = []
    for h in range(N_HEADS):
        o_h = o_scr[:, h * HEAD_V:(h + 1) * HEAD_V]
        z_h = z[:, h * HEAD_V:(h + 1) * HEAD_V]
        o_parts.append(_rms(o_h, og) * (z_h * _sigmoid(z_h)))
    y_b = _mm(jnp.concatenate(o_parts, axis=1), wbout_ref[...])

    gate_a = _mm(xn, win_ref[:, OFF_GA:OFF_GA + D_MODEL])
    gate_b = _mm(xn, win_ref[:, OFF_GB:OFF_GB + D_MODEL])
    mixed = _sigmoid(gate_a) * y_a + _sigmoid(gate_b) * y_b
    h_ref[...] = x + _mm(mixed, wo_ref[...])


def _mlp_kernel(h_ref, g2_ref, wup_ref, wdown_ref, gf_ref, out_ref, *, final_norm):
    h = h_ref[...]
    hn = _rms(h, g2_ref[...]).astype(BF16)
    acc = h
    for j in range(D_FF // FF_CHUNK):
        up = jnp.dot(hn, wup_ref[:, j * FF_CHUNK:(j + 1) * FF_CHUNK], preferred_element_type=F32)
        act = jnp.maximum(up, 0.0)
        acc = acc + _mm(act * act, wdown_ref[j * FF_CHUNK:(j + 1) * FF_CHUNK, :])
    if final_norm:
        acc = _rms(acc, gf_ref[...])
    out_ref[...] = acc


def _resident(shape):
    nd = len(shape)
    return pl.BlockSpec(shape, lambda *_: (0,) * nd, pipeline_mode=pl.Buffered(1))


def _mixer_weight_specs():
    return [
        _resident((1, D_MODEL)),
        _resident((D_MODEL, IN_PACKED)),
        _resident((CONV_A_WIDTH, W_A)),
        _resident((W_A, D_MODEL)),
        _resident((CONV_QKV_WIDTH, W_QKV)),
        _resident((1, LANES)),
        _resident((1, LANES)),
        _resident((1, HEAD_V)),
        _resident((W_VV, D_MODEL)),
        _resident((D_MODEL, D_MODEL)),
    ]


def _mixer_stream(x, buf_a, buf_qkv, s0, weights):
    nb, seq, _ = x.shape
    tm = MIXER_TILE
    chunk = 64
    kern = functools.partial(_mixer_kernel, tm=tm, chunk=chunk, streaming=True)
    per_b3 = lambda b, i: (b, 0, 0)
    per_b4 = lambda b, i: (b, 0, 0, 0)
    return pl.pallas_call(
        kern,
        grid=(nb, seq // tm),
        in_specs=[pl.BlockSpec((None, tm, D_MODEL), lambda b, i: (b, i, 0))] + _mixer_weight_specs() + [
            pl.BlockSpec((None, CONV_A_WIDTH - 1, W_A), per_b3),
            pl.BlockSpec((None, CONV_QKV_WIDTH - 1, W_QKV), per_b3),
            pl.BlockSpec((None, N_HEADS, HEAD_K, HEAD_V), per_b4),
        ],
        out_specs=[
            pl.BlockSpec((None, tm, D_MODEL), lambda b, i: (b, i, 0)),
            pl.BlockSpec((None, CONV_A_WIDTH - 1, W_A), per_b3),
            pl.BlockSpec((None, CONV_QKV_WIDTH - 1, W_QKV), per_b3),
            pl.BlockSpec((None, N_HEADS, HEAD_K, HEAD_V), per_b4),
        ],
        out_shape=[
            jax.ShapeDtypeStruct(x.shape, F32),
            jax.ShapeDtypeStruct(buf_a.shape, F32),
            jax.ShapeDtypeStruct(buf_qkv.shape, F32),
            jax.ShapeDtypeStruct(s0.shape, F32),
        ],
        scratch_shapes=[
            pltpu.VMEM((SUBLANES + tm, W_A), F32),
            pltpu.VMEM((SUBLANES + tm, W_QKV), F32),
            pltpu.VMEM((tm, W_VV), F32),
        ],
        compiler_params=pltpu.CompilerParams(
            dimension_semantics=("arbitrary", "arbitrary"), vmem_limit_bytes=VMEM_LIMIT),
        name="mixer_stream",
    )(x, *weights, buf_a, buf_qkv, s0)


def _mixer_step(x, buf_a, buf_qkv, s0, weights):
    nb, seq, _ = x.shape
    tm = BLOCK_ROWS
    nseq = tm // seq
    kern = functools.partial(_mixer_kernel, tm=tm, chunk=seq, streaming=False)
    grp3 = lambda i: (i, 0, 0)
    grp4 = lambda i: (i, 0, 0, 0)
    x2 = x.reshape(nb * seq, D_MODEL)
    h2, na, nq, ns = pl.pallas_call(
        kern,
        grid=(nb // nseq,),
        in_specs=[pl.BlockSpec((tm, D_MODEL), lambda i: (i, 0))] + _mixer_weight_specs() + [
            pl.BlockSpec((nseq, CONV_A_WIDTH - 1, W_A), grp3),
            pl.BlockSpec((nseq, CONV_QKV_WIDTH - 1, W_QKV), grp3),
            pl.BlockSpec((nseq, N_HEADS, HEAD_K, HEAD_V), grp4),
        ],
        out_specs=[
            pl.BlockSpec((tm, D_MODEL), lambda i: (i, 0)),
            pl.BlockSpec((nseq, CONV_A_WIDTH - 1, W_A), grp3),
            pl.BlockSpec((nseq, CONV_QKV_WIDTH - 1, W_QKV), grp3),
            pl.BlockSpec((nseq, N_HEADS, HEAD_K, HEAD_V), grp4),
        ],
        out_shape=[
            jax.ShapeDtypeStruct(x2.shape, F32),
            jax.ShapeDtypeStruct(buf_a.shape, F32),
            jax.ShapeDtypeStruct(buf_qkv.shape, F32),
            jax.ShapeDtypeStruct(s0.shape, F32),
        ],
        scratch_shapes=[
            pltpu.VMEM((2 * tm, W_A), F32),
            pltpu.VMEM((2 * tm, W_QKV), F32),
            pltpu.VMEM((tm, W_VV), F32),
        ],
        compiler_params=pltpu.CompilerParams(
            dimension_semantics=("arbitrary",), vmem_limit_bytes=VMEM_LIMIT),
        name="mixer_step",
    )(x2, *weights, buf_a, buf_qkv, s0)
    return h2.reshape(x.shape), na, nq, ns


def _mlp(h, g2, w_up, w_down, gf, final_norm):
    shape = h.shape
    h2 = h.reshape(-1, D_MODEL)
    rows = h2.shape[0]
    tm = min(MLP_TILE, rows)
    out = pl.pallas_call(
        functools.partial(_mlp_kernel, final_norm=final_norm),
        grid=(rows // tm,),
        in_specs=[
            pl.BlockSpec((tm, D_MODEL), lambda i: (i, 0)),
            _resident((1, D_MODEL)),
            _resident((D_MODEL, D_FF)),
            _resident((D_FF, D_MODEL)),
            _resident((1, D_MODEL)),
        ],
        out_specs=pl.BlockSpec((tm, D_MODEL), lambda i: (i, 0)),
        out_shape=jax.ShapeDtypeStruct(h2.shape, F32),
        compiler_params=pltpu.CompilerParams(
            dimension_semantics=("arbitrary",), vmem_limit_bytes=VMEM_LIMIT),
        name="mlp",
    )(h2, g2, w_up, w_down, gf)
    return out.reshape(shape)


def _pack_w_in(w):
    n_main = 3 * W_A + W_QKV + W_VV
    main = w[:, :n_main]
    beta = w[:, n_main:n_main + N_HEADS]
    alpha = w[:, n_main + N_HEADS:n_main + 2 * N_HEADS]
    gates = w[:, n_main + 2 * N_HEADS:]
    zpad = jnp.zeros((w.shape[0], LANES - N_HEADS), w.dtype)
    return jnp.concatenate([main, gates, beta, zpad, alpha, zpad], axis=1).astype(BF16)


def _head_row(v):
    return jnp.pad(v.astype(F32), (0, LANES - N_HEADS)).reshape(1, LANES)


def kernel(x_prompt, x_sample, state_conv_a, state_conv_qkv, state_delta, norm1_g, w_in, conv_a_w,
           w_a_out, conv_qkv_w, a_log, dt_bias, onorm_g, w_b_out, w_o, norm2_g, w_up, w_down,
           final_g):
    depth = w_in.shape[0]
    nb = x_prompt.shape[0]
    zero_a = jnp.zeros((nb, CONV_A_WIDTH - 1, W_A), F32)
    zero_qkv = jnp.zeros((nb, CONV_QKV_WIDTH - 1, W_QKV), F32)
    zero_s = jnp.zeros((nb, N_HEADS, HEAD_K, HEAD_V), F32)
    gf = final_g.reshape(1, D_MODEL)

    xp, xs = x_prompt, x_sample
    outs_p = ([], [], [])
    outs_s = ([], [], [])
    for l in range(depth):
        weights = (
            norm1_g[l].reshape(1, D_MODEL), _pack_w_in(w_in[l]), conv_a_w[l], w_a_out[l].astype(BF16),
            conv_qkv_w[l], _head_row(a_log[l]), _head_row(dt_bias[l]), onorm_g[l].reshape(1, HEAD_V),
            w_b_out[l].astype(BF16), w_o[l].astype(BF16))
        g2 = norm2_g[l].reshape(1, D_MODEL)
        wu = w_up[l].astype(BF16)
        wd = w_down[l].astype(BF16)
        last = l == depth - 1

        hp, pa, pq, ps = _mixer_stream(xp, zero_a, zero_qkv, zero_s, weights)
        xp = _mlp(hp, g2, wu, wd, gf, last)
        hs, sa, sq, ss = _mixer_step(xs, state_conv_a[l], state_conv_qkv[l], state_delta[l], weights)
        xs = _mlp(hs, g2, wu, wd, gf, last)
        for acc, val in zip(outs_p, (pa, pq, ps)):
            acc.append(val)
        for acc, val in zip(outs_s, (sa, sq, ss)):
            acc.append(val)

    return (xp, xs,
            jnp.stack(outs_p[0]), jnp.stack(outs_p[1]), jnp.stack(outs_p[2]),
            jnp.stack(outs_s[0]), jnp.stack(outs_s[1]), jnp.stack(outs_s[2]))
```

```python
import functools

import jax
import jax.numpy as jnp
from jax import lax
from jax.experimental import pallas as pl
from jax.experimental.pallas import tpu as pltpu

F32 = jnp.float32
BF16 = jnp.bfloat16

D_MODEL = 1024
W_A = 512
N_HEADS = 4
HEAD_K = 128
HEAD_V = 128
W_QK = N_HEADS * HEAD_K
W_VV = N_HEADS * HEAD_V
W_QKV = 2 * W_QK + W_VV
CONV_A_WIDTH = 3
CONV_QKV_WIDTH = 4
D_FF = 4 * D_MODEL
EPS = 1e-6

LANES = 128
SUBLANES = 8
BLOCK_ROWS = 128

OFF_A = 0
OFF_QKV = OFF_A + 3 * W_A
OFF_Z = OFF_QKV + W_QKV
OFF_GA = OFF_Z + W_VV
OFF_GB = OFF_GA + D_MODEL
OFF_BETA = OFF_GB + D_MODEL
OFF_ALPHA = OFF_BETA + LANES
IN_PACKED = OFF_ALPHA + LANES

PROMPT_CHUNK = 64
MIXER_TILE = 512
MLP_TILE = 512
FF_CHUNK = 1024
VMEM_LIMIT = 56 * 1024 * 1024


def _mm(a, b):
    return jnp.dot(a.astype(BF16), b.astype(BF16), preferred_element_type=F32)


def _mm_nt(a, b):
    return lax.dot_general(a.astype(BF16), b.astype(BF16), (((1,), (1,)), ((), ())),
                           preferred_element_type=F32)


def _mm_tn(a, b):
    return lax.dot_general(a.astype(BF16), b.astype(BF16), (((0,), (0,)), ((), ())),
                           preferred_element_type=F32)


def _rms(x, g):
    return x * lax.rsqrt(jnp.mean(x * x, axis=-1, keepdims=True) + EPS) * g


def _sigmoid(x):
    return 1.0 / (1.0 + jnp.exp(-x))


def _softplus(x):
    return jnp.maximum(x, 0.0) + jnp.log1p(jnp.exp(-jnp.abs(x)))


def _chunk_scan(x, pos, chunk, reverse):
    rows = x.shape[0]
    s = 1
    while s < chunk:
        if reverse:
            x = x + jnp.where(pos < chunk - s, pltpu.roll(x, rows - s, 0), 0.0)
        else:
            x = x + jnp.where(pos >= s, pltpu.roll(x, s, 0), 0.0)
        s *= 2
    return x


def _mixer_kernel(x_ref, g1_ref, win_ref, caw_ref, waout_ref, cqw_ref, alog_ref, dtb_ref, og_ref,
                  wbout_ref, wo_ref, sa_ref, sq_ref, s0_ref,
                  h_ref, na_ref, nq_ref, ns_ref,
                  pad_a, pad_q, o_scr, *, tm, chunk, streaming):
    nseq = tm // chunk
    row = lax.broadcasted_iota(jnp.int32, (tm, LANES), 0)
    pos = row & (chunk - 1)

    if streaming:
        first = pl.program_id(1) == 0

        @pl.when(first)
        def _():
            ns_ref[...] = s0_ref[...]

    def causal_conv(pad, xin, w_ref, width, state_ref, new_ref):
        hw = width - 1
        if streaming:
            @pl.when(first)
            def _():
                pad[SUBLANES - hw:SUBLANES, :] = state_ref[...]
            pad[SUBLANES:SUBLANES + tm, :] = xin
            y = xin * w_ref[hw:hw + 1, :]
            for s in range(1, width):
                y = y + pad[SUBLANES - s:SUBLANES - s + tm, :] * w_ref[hw - s:hw - s + 1, :]
            new_ref[...] = pad[SUBLANES + tm - hw:SUBLANES + tm, :]
            pad[0:SUBLANES, :] = pad[tm:tm + SUBLANES, :]
            return y
        pad[0:tm, :] = xin
        pad[tm:2 * tm, :] = jnp.zeros((tm, xin.shape[1]), F32)
        for q in range(nseq):
            qp = (q - 1) % nseq
            pad[tm + qp * chunk + chunk - hw:tm + qp * chunk + chunk, :] = state_ref[q]
            new_ref[q] = pad[q * chunk + chunk - hw:q * chunk + chunk, :]
        hist = pad[tm:2 * tm, :]
        cpos = lax.broadcasted_iota(jnp.int32, (tm, 1), 0) & (chunk - 1)
        y = xin * w_ref[hw:hw + 1, :]
        for s in range(1, width):
            sh = jnp.where(cpos >= s, pltpu.roll(xin, s, 0), pltpu.roll(hist, s, 0))
            y = y + sh * w_ref[hw - s:hw - s + 1, :]
        return y

    x = x_ref[...]
    xn = _rms(x, g1_ref[...]).astype(BF16)

    p_q = _mm(xn, win_ref[:, OFF_QKV:OFF_QKV + W_QKV])
    gate_a = _mm(xn, win_ref[:, OFF_GA:OFF_GA + D_MODEL])
    gate_b = _mm(xn, win_ref[:, OFF_GB:OFF_GB + D_MODEL])
    c = causal_conv(pad_q, p_q, cqw_ref, CONV_QKV_WIDTH, sq_ref, nq_ref)
    c = c * _sigmoid(c)

    beta = _sigmoid(_mm(xn, win_ref[:, OFF_BETA:OFF_BETA + LANES]))
    a_raw = _mm(xn, win_ref[:, OFF_ALPHA:OFF_ALPHA + LANES])
    z = _mm(xn, win_ref[:, OFF_Z:OFF_Z + W_VV])
    p_a = _mm(xn, win_ref[:, OFF_A:OFF_A + 3 * W_A])
    g = -jnp.exp(alog_ref[...]) * _softplus(a_raw + dtb_ref[...])
    gc = _chunk_scan(g, pos, chunk, reverse=False)
    g_after = _chunk_scan(g, pos, chunk, reverse=True) - g
    eg = jnp.exp(gc)
    e_after = jnp.exp(g_after)
    gc_t = gc.T

    ri = lax.broadcasted_iota(jnp.int32, (BLOCK_ROWS, BLOCK_ROWS), 0)
    ci = lax.broadcasted_iota(jnp.int32, (BLOCK_ROWS, BLOCK_ROWS), 1)
    m_incl = ((ri & -chunk) == (ci & -chunk)) & (ri >= ci)
    eye = jnp.where(ri == ci, 1.0, 0.0)
    m_levels = []
    size = 1
    while size < chunk:
        m_levels.append(((ri & -(2 * size)) == (ci & -(2 * size))) & ((ri & -size) > (ci & -size)))
        size *= 2

    pairs = [(rb, h) for rb in range(tm // BLOCK_ROWS) for h in range(N_HEADS)]
    q_g, k_t, rhs, p_blk, a_blk, t_inv = {}, {}, {}, {}, {}, {}
    for rb, h in pairs:
        rows = slice(rb * BLOCK_ROWS, (rb + 1) * BLOCK_ROWS)
        qh = c[rows, h * HEAD_K:(h + 1) * HEAD_K]
        kh = c[rows, W_QK + h * HEAD_K:W_QK + (h + 1) * HEAD_K]
        vh = c[rows, 2 * W_QK + h * HEAD_V:2 * W_QK + (h + 1) * HEAD_V]
        qh = qh * (lax.rsqrt(jnp.sum(qh * qh, axis=-1, keepdims=True) + EPS) * (HEAD_K ** -0.5))
        kh = kh * lax.rsqrt(jnp.sum(kh * kh, axis=-1, keepdims=True) + EPS)
        b_col = beta[rows, h:h + 1]
        eg_col = eg[rows, h:h + 1]
        diff = gc[rows, h:h + 1] - gc_t[h:h + 1, rows]
        dec_incl = jnp.exp(jnp.where(m_incl, diff, -1e30))
        qk = _mm_nt(jnp.concatenate([qh, kh], axis=0), kh)
        p_blk[rb, h] = qk[0:BLOCK_ROWS] * dec_incl
        a_blk[rb, h] = qk[BLOCK_ROWS:] * dec_incl * b_col
        rhs[rb, h] = jnp.concatenate([vh * b_col, kh * (b_col * eg_col)], axis=1)
        q_g[rb, h] = qh * eg_col
        k_t[rb, h] = kh * e_after[rows, h:h + 1]
        t_inv[rb, h] = eye - jnp.where(m_levels[0], a_blk[rb, h], 0.0)

    for m_off in m_levels[1:]:
        a_t = {p: _mm(jnp.where(m_off, a_blk[p], 0.0), t_inv[p]) for p in pairs}
        t_inv = {p: t_inv[p] - _mm(t_inv[p], a_t[p]) for p in pairs}
    w_vk = {p: _mm(t_inv[p], rhs[p]) for p in pairs}

    u = causal_conv(pad_a, p_a[:, W_A:2 * W_A] * p_a[:, 2 * W_A:3 * W_A], caw_ref, CONV_A_WIDTH,
                    sa_ref, na_ref)
    y_a = _mm(p_a[:, 0:W_A] * u, waout_ref[...])

    cpb = BLOCK_ROWS // chunk
    steps = [[(q, h) for h in range(N_HEADS)] for q in range(tm // chunk)] if streaming else \
        [[(q, h) for q in range(tm // chunk) for h in range(N_HEADS)]]
    for step in steps:
        s_old, xs, u_c = {}, {}, {}
        for q, h in step:
            p, c0 = (q // cpb, h), (q % cpb) * chunk
            s_old[q, h] = ns_ref[h] if streaming else s0_ref[q, h]
            xs[q, h] = _mm(jnp.concatenate([w_vk[p][c0:c0 + chunk, HEAD_V:], q_g[p][c0:c0 + chunk]],
                                           axis=0), s_old[q, h])
        for q, h in step:
            p, c0 = (q // cpb, h), (q % cpb) * chunk
            u_c[q, h] = w_vk[p][c0:c0 + chunk, 0:HEAD_V] - xs[q, h][0:chunk]
            o_c = xs[q, h][chunk:] + _mm(p_blk[p][c0:c0 + chunk, c0:c0 + chunk], u_c[q, h])
            o_scr[q * chunk:(q + 1) * chunk, h * HEAD_V:(h + 1) * HEAD_V] = o_c
        for q, h in step:
            p, c0 = (q // cpb, h), (q % cpb) * chunk
            g_last = eg[(q + 1) * chunk - 1:(q + 1) * chunk, h:h + 1]
            s_new = s_old[q, h] * g_last + _mm_tn(k_t[p][c0:c0 + chunk], u_c[q, h])
            if streaming:
                ns_ref[h] = s_new
            else:
                ns_ref[q, h] = s_new

    og = og_ref[...]
    o_parts = []
    for h in range(N_HEADS):
        o_h = o_scr[:, h * HEAD_V:(h + 1) * HEAD_V]
        z_h = z[:, h * HEAD_V:(h + 1) * HEAD_V]
        o_parts.append(_rms(o_h, og) * (z_h * _sigmoid(z_h)))
    y_b = _mm(jnp.concatenate(o_parts, axis=1), wbout_ref[...])

    mixed =_sigmoid(gate_a) * y_a + _sigmoid(gate_b) * y_b
    h_ref[...] = x + _mm(mixed, wo_ref[...])


def _mlp_kernel(h_ref, g2_ref, wup_ref, wdown_ref, gf_ref, out_ref, *, final_norm):
    h = h_ref[...]
    hn = _rms(h, g2_ref[...]).astype(BF16)
    acc = h
    for j in range(D_FF // FF_CHUNK):
        up = jnp.dot(hn, wup_ref[:, j * FF_CHUNK:(j + 1) * FF_CHUNK], preferred_element_type=F32)
        act = jnp.maximum(up, 0.0)
        acc = acc + _mm(act * act, wdown_ref[j * FF_CHUNK:(j + 1) * FF_CHUNK, :])
    if final_norm:
        acc = _rms(acc, gf_ref[...])
    out_ref[...] = acc


def _resident(shape, layer):
    nd = len(shape)
    return pl.BlockSpec((None,) + shape, lambda *_: (layer,) + (0,) * nd,
                        pipeline_mode=pl.Buffered(1))


def _mixer_weight_specs(layer):
    return [
        _resident((1, D_MODEL), layer),
        _resident((D_MODEL, IN_PACKED), layer),
        _resident((CONV_A_WIDTH, W_A), layer),
        _resident((W_A, D_MODEL), layer),
        _resident((CONV_QKV_WIDTH, W_QKV), layer),
        _resident((1, LANES), layer),
        _resident((1, LANES), layer),
        _resident((1, HEAD_V), layer),
        _resident((W_VV, D_MODEL), layer),
        _resident((D_MODEL, D_MODEL), layer),
    ]


def _mixer_stream(x, buf_a, buf_qkv, s0, weights, layer):
    nb, seq, _ = x.shape
    tm = MIXER_TILE
    kern = functools.partial(_mixer_kernel, tm=tm, chunk=PROMPT_CHUNK, streaming=True)
    per_b3 = lambda b, i: (b, 0, 0)
    per_b4 = lambda b, i: (b, 0, 0, 0)
    return pl.pallas_call(
        kern,
        grid=(nb, seq // tm),
        in_specs=[pl.BlockSpec((None, tm, D_MODEL), lambda b, i: (b, i, 0))]
        + _mixer_weight_specs(layer) + [
            pl.BlockSpec((None, CONV_A_WIDTH - 1, W_A), per_b3),
            pl.BlockSpec((None, CONV_QKV_WIDTH - 1, W_QKV), per_b3),
            pl.BlockSpec((None, N_HEADS, HEAD_K, HEAD_V), per_b4),
        ],
        out_specs=[
            pl.BlockSpec((None, tm, D_MODEL), lambda b, i: (b, i, 0)),
            pl.BlockSpec((None, CONV_A_WIDTH - 1, W_A), per_b3),
            pl.BlockSpec((None, CONV_QKV_WIDTH - 1, W_QKV), per_b3),
            pl.BlockSpec((None, N_HEADS, HEAD_K, HEAD_V), per_b4),
        ],
        out_shape=[
            jax.ShapeDtypeStruct(x.shape, F32),
            jax.ShapeDtypeStruct(buf_a.shape, F32),
            jax.ShapeDtypeStruct(buf_qkv.shape, F32),
            jax.ShapeDtypeStruct(s0.shape, F32),
        ],
        scratch_shapes=[
            pltpu.VMEM((SUBLANES + tm, W_A), F32),
            pltpu.VMEM((SUBLANES + tm, W_QKV), F32),
            pltpu.VMEM((tm, W_VV), F32),
        ],
        compiler_params=pltpu.CompilerParams(
            dimension_semantics=("arbitrary", "arbitrary"), vmem_limit_bytes=VMEM_LIMIT),
        name="mixer_stream",
    )(x, *weights, buf_a, buf_qkv, s0)


def _mixer_step(x, buf_a, buf_qkv, s0, weights, layer):
    nb, seq, _ = x.shape
    tm = BLOCK_ROWS
    nseq = tm // seq
    kern = functools.partial(_mixer_kernel, tm=tm, chunk=seq, streaming=False)
    grp3 = lambda i: (i, 0, 0)
    grp4 = lambda i: (i, 0, 0, 0)
    lay3 = lambda i: (layer, i, 0, 0)
    lay4 = lambda i: (layer, i, 0, 0, 0)
    x2 = x.reshape(nb * seq, D_MODEL)
    h2, na, nq, ns = pl.pallas_call(
        kern,
        grid=(nb // nseq,),
        in_specs=[pl.BlockSpec((tm, D_MODEL), lambda i: (i, 0))] + _mixer_weight_specs(layer) + [
            pl.BlockSpec((None, nseq, CONV_A_WIDTH - 1, W_A), lay3),
            pl.BlockSpec((None, nseq, CONV_QKV_WIDTH - 1, W_QKV), lay3),
            pl.BlockSpec((None, nseq, N_HEADS, HEAD_K, HEAD_V), lay4),
        ],
        out_specs=[
            pl.BlockSpec((tm, D_MODEL), lambda i: (i, 0)),
            pl.BlockSpec((nseq, CONV_A_WIDTH - 1, W_A), grp3),
            pl.BlockSpec((nseq, CONV_QKV_WIDTH - 1, W_QKV), grp3),
            pl.BlockSpec((nseq, N_HEADS, HEAD_K, HEAD_V), grp4),
        ],
        out_shape=[
            jax.ShapeDtypeStruct(x2.shape, F32),
            jax.ShapeDtypeStruct(buf_a.shape[1:], F32),
            jax.ShapeDtypeStruct(buf_qkv.shape[1:], F32),
            jax.ShapeDtypeStruct(s0.shape[1:], F32),
        ],
        scratch_shapes=[
            pltpu.VMEM((2 * tm, W_A), F32),
            pltpu.VMEM((2 * tm, W_QKV), F32),
            pltpu.VMEM((tm, W_VV), F32),
        ],
        compiler_params=pltpu.CompilerParams(
            dimension_semantics=("arbitrary",), vmem_limit_bytes=VMEM_LIMIT),
        name="mixer_step",
    )(x2, *weights, buf_a, buf_qkv, s0)
    return h2.reshape(x.shape), na, nq, ns


def _mlp(h, g2, w_up, w_down, gf, layer, final_norm):
    shape = h.shape
    h2 = h.reshape(-1, D_MODEL)
    rows = h2.shape[0]
    tm = min(MLP_TILE, rows)
    out = pl.pallas_call(
        functools.partial(_mlp_kernel, final_norm=final_norm),
        grid=(rows // tm,),
        in_specs=[
            pl.BlockSpec((tm, D_MODEL), lambda i: (i, 0)),
            _resident((1, D_MODEL), layer),
            _resident((D_MODEL, D_FF), layer),
            _resident((D_FF, D_MODEL), layer),
            _resident((1, D_MODEL), 0),
        ],
        out_specs=pl.BlockSpec((tm, D_MODEL), lambda i: (i, 0)),
        out_shape=jax.ShapeDtypeStruct(h2.shape, F32),
        compiler_params=pltpu.CompilerParams(
            dimension_semantics=("arbitrary",), vmem_limit_bytes=VMEM_LIMIT),
        name="mlp",
    )(h2, g2, w_up, w_down, gf)
    return out.reshape(shape)


def _pack_w_in(w):
    n_main = 3 * W_A + W_QKV + W_VV
    w = w.astype(BF16)
    main = w[..., :n_main]
    beta = w[..., n_main:n_main + N_HEADS]
    alpha = w[..., n_main + N_HEADS:n_main + 2 * N_HEADS]
    gates = w[..., n_main + 2 * N_HEADS:]
    zpad = jnp.zeros(w.shape[:-1] + (LANES - N_HEADS,), BF16)
    return jnp.concatenate([main, gates, beta, zpad, alpha, zpad], axis=-1)


def _head_rows(v):
    return jnp.pad(v.astype(F32), ((0, 0), (0, LANES - N_HEADS)))[:, None, :]


def kernel(x_prompt, x_sample, state_conv_a, state_conv_qkv, state_delta, norm1_g, w_in, conv_a_w,
           w_a_out, conv_qkv_w, a_log, dt_bias, onorm_g, w_b_out, w_o, norm2_g, w_up, w_down,
           final_g):
    depth = w_in.shape[0]
    nb = x_prompt.shape[0]
    zero_a = jnp.zeros((nb, CONV_A_WIDTH - 1, W_A), F32)
    zero_qkv = jnp.zeros((nb, CONV_QKV_WIDTH - 1, W_QKV), F32)
    zero_s = jnp.zeros((nb, N_HEADS, HEAD_K, HEAD_V), F32)

    weights = (
        norm1_g[:, None, :], _pack_w_in(w_in), conv_a_w, w_a_out.astype(BF16), conv_qkv_w,
        _head_rows(a_log), _head_rows(dt_bias), onorm_g[:, None, :], w_b_out.astype(BF16),
        w_o.astype(BF16))
    g2 = norm2_g[:, None, :]
    wu = w_up.astype(BF16)
    wd = w_down.astype(BF16)
    gf = final_g.reshape(1, 1, D_MODEL)

    xp, xs = x_prompt, x_sample
    outs_p = ([], [], [])
    outs_s = ([], [], [])
    for l in range(depth):
        last = l == depth - 1
        hp, pa, pq, ps = _mixer_stream(xp, zero_a, zero_qkv, zero_s, weights, l)
        xp = _mlp(hp, g2, wu, wd, gf, l, last)
        hs, sa, sq, ss = _mixer_step(xs, state_conv_a, state_conv_qkv, state_delta, weights, l)
        xs = _mlp(hs, g2, wu, wd, gf, l, last)
        for acc, val in zip(outs_p, (pa, pq, ps)):
            acc.append(val)
        for acc, val in zip(outs_s, (sa, sq, ss)):
            acc.append(val)

    return (xp, xs,
            jnp.stack(outs_p[0]), jnp.stack(outs_p[1]), jnp.stack(outs_p[2]),
            jnp.stack(outs_s[0]), jnp.stack(outs_s[1]), jnp.stack(outs_s[2]))
```

```python
import functools

import jax
import jax.numpy as jnp
from jax import lax
from jax.experimental import pallas as pl
from jax.experimental.pallas import tpu as pltpu

F32 = jnp.float32
BF16 = jnp.bfloat16

D_MODEL = 1024
W_A = 512
N_HEADS = 4
HEAD_K = 128
HEAD_V = 128
W_QK = N_HEADS * HEAD_K
W_VV = N_HEADS * HEAD_V
W_QKV = 2 * W_QK + W_VV
CONV_A_WIDTH = 3
CONV_QKV_WIDTH = 4
D_FF = 4 * D_MODEL
EPS = 1e-6

LANES = 128
SUBLANES = 8
BLOCK_ROWS = 128

OFF_A = 0
OFF_QKV = OFF_A + 3 * W_A
OFF_Z = OFF_QKV + W_QKV
OFF_GA = OFF_Z + W_VV
OFF_GB = OFF_GA + D_MODEL
OFF_BETA = OFF_GB + D_MODEL
OFF_ALPHA = OFF_BETA + LANES
IN_PACKED = OFF_ALPHA + LANES

PROMPT_CHUNK = 64
MIXER_TILE = 512
MLP_TILE = 512
FF_CHUNK = 1024
PIECE = 256
VMEM_LIMIT = 56 * 1024 * 1024


def _mm(a, b):
    return jnp.dot(a.astype(BF16), b.astype(BF16), preferred_element_type=F32)


def _mm_nt(a, b):
    return lax.dot_general(a.astype(BF16), b.astype(BF16), (((1,), (1,)), ((), ())),
                           preferred_element_type=F32)


def _mm_tn(a, b):
    return lax.dot_general(a.astype(BF16), b.astype(BF16), (((0,), (0,)), ((), ())),
                           preferred_element_type=F32)


def _rms(x, g):
    return x * lax.rsqrt(jnp.mean(x * x, axis=-1, keepdims=True) + EPS) * g


def _sigmoid(x):
    return 1.0 / (1.0 + jnp.exp(-x))


def _softplus(x):
    return jnp.maximum(x, 0.0) + jnp.log1p(jnp.exp(-jnp.abs(x)))


def _chunk_scan(x, pos, chunk, reverse):
    rows = x.shape[0]
    s = 1
    while s < chunk:
        if reverse:
            x = x + jnp.where(pos < chunk - s, pltpu.roll(x, rows - s, 0), 0.0)
        else:
            x = x + jnp.where(pos >= s, pltpu.roll(x, s, 0), 0.0)
        s *= 2
    return x


def _mixer_kernel(x_ref, g1_ref, win_ref, caw_ref, waout_ref, cqw_ref, alog_ref, dtb_ref, og_ref,
                  wbout_ref, wo_ref, sa_ref, sq_ref, s0_ref,
                  h_ref, na_ref, nq_ref, ns_ref,
                  pad_a, pad_q, o_scr, *, tm, chunk, streaming):
    nseq = tm // chunk
    row = lax.broadcasted_iota(jnp.int32, (tm, LANES), 0)
    pos = row & (chunk - 1)

    if streaming:
        first = pl.program_id(1) == 0

        @pl.when(first)
        def _():
            ns_ref[...] = s0_ref[...]
            pad_a[SUBLANES - (CONV_A_WIDTH - 1):SUBLANES, :] = sa_ref[...]
            pad_q[SUBLANES - (CONV_QKV_WIDTH - 1):SUBLANES, :] = sq_ref[...]

    fillers = []

    def fill(n=1):
        for _ in range(min(n, len(fillers))):
            fillers.pop(0)()

    def queue_projection(lhs, w_ref, col0, width, out):
        def piece(j):
            def run():
                out[j] = _mm(lhs(), w_ref[:, col0 + j * PIECE:col0 + (j + 1) * PIECE])
            return run
        fillers.extend(piece(j) for j in range(width // PIECE))

    def causal_conv(pad, xin, w_ref, width, state_ref, new_ref, act, fills):
        hw = width - 1
        taps = [w_ref[j:j + 1, :] for j in range(width)]
        if streaming:
            pad[SUBLANES:SUBLANES + tm, :] = xin
            outs = []
            for r0 in range(0, tm, BLOCK_ROWS):
                fill(fills)
                y = xin[r0:r0 + BLOCK_ROWS] * taps[hw]
                for s in range(1, width):
                    y = y + pad[SUBLANES - s + r0:SUBLANES - s + r0 + BLOCK_ROWS, :] * taps[hw - s]
                outs.append(act(y))
            new_ref[...] = pad[SUBLANES + tm - hw:SUBLANES + tm, :]
            pad[0:SUBLANES, :] = pad[tm:tm + SUBLANES, :]
            return jnp.concatenate(outs, axis=0)
        pad[0:tm, :] = xin
        pad[tm:2 * tm, :] = jnp.zeros((tm, xin.shape[1]), F32)
        for q in range(nseq):
            qp = (q - 1) % nseq
            pad[tm + qp * chunk + chunk - hw:tm + qp * chunk + chunk, :] = state_ref[q]
            new_ref[q] = pad[q * chunk + chunk - hw:q * chunk + chunk, :]
        hist = pad[tm:2 * tm, :]
        cpos = lax.broadcasted_iota(jnp.int32, (tm, 1), 0) & (chunk - 1)
        fill(fills)
        y = xin * taps[hw]
        for s in range(1, width):
            sh = jnp.where(cpos >= s, pltpu.roll(xin, s, 0), pltpu.roll(hist, s, 0))
            y = y + sh * taps[hw - s]
        return act(y)

    x = x_ref[...]
    xn = _rms(x, g1_ref[...]).astype(BF16)

    pa_cols = [None] * (3 * W_A // PIECE)
    z_cols = [None] * (W_VV // PIECE)
    gate_cols = [None] * (2 * D_MODEL // PIECE)
    ya_cols = [None] * (D_MODEL // PIECE)
    queue_projection(lambda: xn, win_ref, OFF_A, 3 * W_A, pa_cols)
    queue_projection(lambda: xn, win_ref, OFF_Z, W_VV, z_cols)
    queue_projection(lambda: xn, win_ref, OFF_GA, 2 * D_MODEL, gate_cols)

    p_q = _mm(xn, win_ref[:, OFF_QKV:OFF_QKV + W_QKV])
    beta = _sigmoid(_mm(xn, win_ref[:, OFF_BETA:OFF_BETA + LANES]))
    a_raw = _mm(xn, win_ref[:, OFF_ALPHA:OFF_ALPHA + LANES])
    c = causal_conv(pad_q, p_q, cqw_ref, CONV_QKV_WIDTH, sq_ref, nq_ref, lambda y: y * _sigmoid(y), 0)
    fill(8)

    g = -jnp.exp(alog_ref[...]) * _softplus(a_raw + dtb_ref[...])
    gc = _chunk_scan(g, pos, chunk, reverse=False)
    g_after = _chunk_scan(g, pos, chunk, reverse=True) - g
    eg = jnp.exp(gc)
    e_after = jnp.exp(g_after)
    gc_t = gc.T

    ri = lax.broadcasted_iota(jnp.int32, (BLOCK_ROWS, BLOCK_ROWS), 0)
    ci = lax.broadcasted_iota(jnp.int32, (BLOCK_ROWS, BLOCK_ROWS), 1)
    m_incl = ((ri & -chunk) == (ci & -chunk)) & (ri >= ci)
    eye = jnp.where(ri == ci, 1.0, 0.0)
    m_levels = []
    size = 1
    while size < chunk:
        m_levels.append(((ri & -(2 * size)) == (ci & -(2 * size))) & ((ri & -size) > (ci & -size)))
        size *= 2

    pairs = [(rb, h) for rb in range(tm // BLOCK_ROWS) for h in range(N_HEADS)]
    q_g, k_t, rhs, p_blk, a_blk, t_inv = {}, {}, {}, {}, {}, {}
    for rb, h in pairs:
        rows = slice(rb * BLOCK_ROWS, (rb + 1) * BLOCK_ROWS)
        qh = c[rows, h * HEAD_K:(h + 1) * HEAD_K]
        kh = c[rows, W_QK + h * HEAD_K:W_QK + (h + 1) * HEAD_K]
        vh = c[rows, 2 * W_QK + h * HEAD_V:2 * W_QK + (h + 1) * HEAD_V]
        qh = qh * (lax.rsqrt(jnp.sum(qh * qh, axis=-1, keepdims=True) + EPS) * (HEAD_K ** -0.5))
        kh = kh * lax.rsqrt(jnp.sum(kh * kh, axis=-1, keepdims=True) + EPS)
        b_col = beta[rows, h:h + 1]
        eg_col = eg[rows, h:h + 1]
        diff = gc[rows, h:h + 1] - gc_t[h:h + 1, rows]
        dec_incl = jnp.exp(jnp.where(m_incl, diff, -1e30))
        qk = _mm_nt(jnp.concatenate([qh, kh], axis=0), kh)
        p_blk[rb, h] = qk[0:BLOCK_ROWS] * dec_incl
        a_blk[rb, h] = qk[BLOCK_ROWS:] * dec_incl * b_col
        rhs[rb, h] = jnp.concatenate([vh * b_col, kh * (b_col * eg_col)], axis=1)
        q_g[rb, h] = qh * eg_col
        k_t[rb, h] = kh * e_after[rows, h:h + 1]
        t_inv[rb, h] = eye - jnp.where(m_levels[0], a_blk[rb, h], 0.0)

    fill(D_MODEL // PIECE)

    p_a = jnp.concatenate(pa_cols, axis=1)
    gated_u = causal_conv(pad_a, p_a[:, W_A:2 * W_A] * p_a[:, 2 * W_A:3 * W_A], caw_ref,
                          CONV_A_WIDTH, sa_ref, na_ref, lambda y: y, 0) * p_a[:, 0:W_A]
    gated_u = gated_u.astype(BF16)
    queue_projection(lambda: gated_u, waout_ref, 0, D_MODEL, ya_cols)
    z = jnp.concatenate(z_cols, axis=1)
    z_act = z * _sigmoid(z)
    sig_a = _sigmoid(jnp.concatenate(gate_cols[:D_MODEL // PIECE], axis=1))

    for m_off in m_levels[1:]:
        a_t = {p: _mm(jnp.where(m_off, a_blk[p], 0.0), t_inv[p]) for p in pairs}
        t_inv = {p: t_inv[p] - _mm(t_inv[p], a_t[p]) for p in pairs}
    w_vk = {p: _mm(t_inv[p], rhs[p]) for p in pairs}

    cpb = BLOCK_ROWS // chunk
    steps = [[(q, h) for h in range(N_HEADS)] for q in range(tm // chunk)] if streaming else \
        [[(q, h) for q in range(tm // chunk) for h in range(N_HEADS)]]
    for step in steps:
        s_old, xs, u_c = {}, {}, {}
        for q, h in step:
            p, c0 = (q // cpb, h), (q % cpb) * chunk
            s_old[q, h] = ns_ref[h] if streaming else s0_ref[q, h]
            xs[q, h] = _mm(jnp.concatenate([w_vk[p][c0:c0 + chunk, HEAD_V:], q_g[p][c0:c0 + chunk]],
                                           axis=0), s_old[q, h])
        fill()
        for q, h in step:
            p, c0 = (q // cpb, h), (q % cpb) * chunk
            u_c[q, h] = w_vk[p][c0:c0 + chunk, 0:HEAD_V] - xs[q, h][0:chunk]
            o_c = xs[q, h][chunk:] + _mm(p_blk[p][c0:c0 + chunk, c0:c0 + chunk], u_c[q, h])
            o_scr[q * chunk:(q + 1) * chunk, h * HEAD_V:(h + 1) * HEAD_V] = o_c
        for q, h in step:
            p, c0 = (q // cpb, h), (q % cpb) * chunk
            g_last = eg[(q + 1) * chunk - 1:(q + 1) * chunk, h:h + 1]
            s_new = s_old[q, h] * g_last + _mm_tn(k_t[p][c0:c0 + chunk], u_c[q, h])
            if streaming:
                ns_ref[h] = s_new
            else:
                ns_ref[q, h] = s_new
    fill(len(fillers))

    og = og_ref[...]
    o_parts = []
    for h in range(N_HEADS):
        o_h = o_scr[:, h * HEAD_V:(h + 1) * HEAD_V]
        o_parts.append(_rms(o_h, og) * z_act[:, h * HEAD_V:(h + 1) * HEAD_V])
    y_b = _mm(jnp.concatenate(o_parts, axis=1), wbout_ref[...])

    sig_b = _sigmoid(jnp.concatenate(gate_cols[D_MODEL // PIECE:], axis=1))
    mixed = sig_a * jnp.concatenate(ya_cols, axis=1) + sig_b * y_b
    h_ref[...] = x + _mm(mixed, wo_ref[...])


def _mlp_kernel(h_ref, g2_ref, wup_ref, wdown_ref, gf_ref, out_ref, *, final_norm):
    h = h_ref[...]
    hn = _rms(h, g2_ref[...]).astype(BF16)
    acc = h
    for j in range(D_FF // FF_CHUNK):
        up = jnp.dot(hn, wup_ref[:, j * FF_CHUNK:(j + 1) * FF_CHUNK], preferred_element_type=F32)
        act = jnp.maximum(up, 0.0)
        acc = acc + _mm(act * act, wdown_ref[j * FF_CHUNK:(j + 1) * FF_CHUNK, :])
    if final_norm:
        acc = _rms(acc, gf_ref[...])
    out_ref[...] = acc


def _resident(shape, layer):
    nd = len(shape)
    return pl.BlockSpec((None,) + shape, lambda *_: (layer,) + (0,) * nd,
                        pipeline_mode=pl.Buffered(1))


def _mixer_weight_specs(layer):
    return [
        _resident((1, D_MODEL), layer),
        _resident((D_MODEL, IN_PACKED), layer),
        _resident((CONV_A_WIDTH, W_A), layer),
        _resident((W_A, D_MODEL), layer),
        _resident((CONV_QKV_WIDTH, W_QKV), layer),
        _resident((1, LANES), layer),
        _resident((1, LANES), layer),
        _resident((1, HEAD_V), layer),
        _resident((W_VV, D_MODEL), layer),
        _resident((D_MODEL, D_MODEL), layer),
    ]


def _mixer_stream(x, buf_a, buf_qkv, s0, weights, layer):
    nb, seq, _ = x.shape
    tm = MIXER_TILE
    kern = functools.partial(_mixer_kernel, tm=tm, chunk=PROMPT_CHUNK, streaming=True)
    per_b3 = lambda b, i: (b, 0, 0)
    per_b4 = lambda b, i: (b, 0, 0, 0)
    return pl.pallas_call(
        kern,
        grid=(nb, seq // tm),
        in_specs=[pl.BlockSpec((None, tm, D_MODEL), lambda b, i: (b, i, 0))]
        + _mixer_weight_specs(layer) + [
            pl.BlockSpec((None, CONV_A_WIDTH - 1, W_A), per_b3),
            pl.BlockSpec((None, CONV_QKV_WIDTH - 1, W_QKV), per_b3),
            pl.BlockSpec((None, N_HEADS, HEAD_K, HEAD_V), per_b4),
        ],
        out_specs=[
            pl.BlockSpec((None, tm, D_MODEL), lambda b, i: (b, i, 0)),
            pl.BlockSpec((None, CONV_A_WIDTH - 1, W_A), per_b3),
            pl.BlockSpec((None, CONV_QKV_WIDTH - 1, W_QKV), per_b3),
            pl.BlockSpec((None, N_HEADS, HEAD_K, HEAD_V), per_b4),
        ],
        out_shape=[
            jax.ShapeDtypeStruct(x.shape, F32),
            jax.ShapeDtypeStruct(buf_a.shape, F32),
            jax.ShapeDtypeStruct(buf_qkv.shape, F32),
            jax.ShapeDtypeStruct(s0.shape, F32),
        ],
        scratch_shapes=[
            pltpu.VMEM((SUBLANES + tm, W_A), F32),
            pltpu.VMEM((SUBLANES + tm, W_QKV), F32),
            pltpu.VMEM((tm, W_VV), F32),
        ],
        compiler_params=pltpu.CompilerParams(
            dimension_semantics=("arbitrary", "arbitrary"), vmem_limit_bytes=VMEM_LIMIT),
        name="mixer_stream",
    )(x, *weights, buf_a, buf_qkv, s0)


def _mixer_step(x, buf_a, buf_qkv, s0, weights, layer):
    nb, seq, _ = x.shape
    tm = BLOCK_ROWS
    nseq = tm // seq
    kern = functools.partial(_mixer_kernel, tm=tm, chunk=seq, streaming=False)
    grp3 = lambda i: (i, 0, 0)
    grp4 = lambda i: (i, 0, 0, 0)
    lay3 = lambda i: (layer, i, 0, 0)
    lay4 = lambda i: (layer, i, 0, 0, 0)
    x2 = x.reshape(nb * seq, D_MODEL)
    h2, na, nq, ns = pl.pallas_call(
        kern,
        grid=(nb // nseq,),
        in_specs=[pl.BlockSpec((tm, D_MODEL), lambda i: (i, 0))] + _mixer_weight_specs(layer) + [
            pl.BlockSpec((None, nseq, CONV_A_WIDTH - 1, W_A), lay3),
            pl.BlockSpec((None, nseq, CONV_QKV_WIDTH - 1, W_QKV), lay3),
            pl.BlockSpec((None, nseq, N_HEADS, HEAD_K, HEAD_V), lay4),
        ],
        out_specs=[
            pl.BlockSpec((tm, D_MODEL), lambda i: (i, 0)),
            pl.BlockSpec((nseq, CONV_A_WIDTH - 1, W_A), grp3),
            pl.BlockSpec((nseq, CONV_QKV_WIDTH - 1, W_QKV), grp3),
            pl.BlockSpec((nseq, N_HEADS, HEAD_K, HEAD_V), grp4),
        ],
        out_shape=[
            jax.ShapeDtypeStruct(x2.shape, F32),
            jax.ShapeDtypeStruct(buf_a.shape[1:], F32),
            jax.ShapeDtypeStruct(buf_qkv.shape[1:], F32),
            jax.ShapeDtypeStruct(s0.shape[1:], F32),
        ],
        scratch_shapes=[
            pltpu.VMEM((2 * tm, W_A), F32),
            pltpu.VMEM((2 * tm, W_QKV), F32),
            pltpu.VMEM((tm, W_VV), F32),
        ],
        compiler_params=pltpu.CompilerParams(
            dimension_semantics=("arbitrary",), vmem_limit_bytes=VMEM_LIMIT),
        name="mixer_step",
    )(x2, *weights, buf_a, buf_qkv, s0)
    return h2.reshape(x.shape), na, nq, ns


def _mlp(h, g2, w_up, w_down, gf, layer, final_norm):
    shape = h.shape
    h2 = h.reshape(-1, D_MODEL)
    rows = h2.shape[0]
    tm = min(MLP_TILE, rows)
    out = pl.pallas_call(
        functools.partial(_mlp_kernel, final_norm=final_norm),
        grid=(rows // tm,),
        in_specs=[
            pl.BlockSpec((tm, D_MODEL), lambda i: (i, 0)),
            _resident((1, D_MODEL), layer),
            _resident((D_MODEL, D_FF), layer),
            _resident((D_FF, D_MODEL), layer),
            _resident((1, D_MODEL), 0),
        ],
        out_specs=pl.BlockSpec((tm, D_MODEL), lambda i: (i, 0)),
        out_shape=jax.ShapeDtypeStruct(h2.shape, F32),
        compiler_params=pltpu.CompilerParams(
            dimension_semantics=("arbitrary",), vmem_limit_bytes=VMEM_LIMIT),
        name="mlp",
    )(h2, g2, w_up, w_down, gf)
    return out.reshape(shape)


def _pack_w_in(w):
    n_main = 3 * W_A + W_QKV + W_VV
    w = w.astype(BF16)
    main = w[..., :n_main]
    beta = w[..., n_main:n_main + N_HEADS]
    alpha = w[..., n_main + N_HEADS:n_main + 2 * N_HEADS]
    gates = w[..., n_main + 2 * N_HEADS:]
    zpad = jnp.zeros(w.shape[:-1] + (LANES - N_HEADS,), BF16)
    return jnp.concatenate([main, gates, beta, zpad, alpha, zpad], axis=-1)


def _head_rows(v):
    return jnp.pad(v.astype(F32), ((0, 0), (0, LANES - N_HEADS)))[:, None, :]


def kernel(x_prompt, x_sample, state_conv_a, state_conv_qkv, state_delta, norm1_g, w_in, conv_a_w,
           w_a_out, conv_qkv_w, a_log, dt_bias, onorm_g, w_b_out, w_o, norm2_g, w_up, w_down,
           final_g):
    depth = w_in.shape[0]
    nb = x_prompt.shape[0]
    zero_a = jnp.zeros((nb, CONV_A_WIDTH - 1, W_A), F32)
    zero_qkv = jnp.zeros((nb, CONV_QKV_WIDTH - 1, W_QKV), F32)
    zero_s = jnp.zeros((nb, N_HEADS, HEAD_K, HEAD_V), F32)

    weights = (
        norm1_g[:, None, :], _pack_w_in(w_in), conv_a_w, w_a_out.astype(BF16), conv_qkv_w,
        _head_rows(a_log), _head_rows(dt_bias), onorm_g[:, None, :], w_b_out.astype(BF16),
        w_o.astype(BF16))
    g2 = norm2_g[:, None, :]
    wu = w_up.astype(BF16)
    wd = w_down.astype(BF16)
    gf = final_g.reshape(1, 1, D_MODEL)

    xp, xs = x_prompt, x_sample
    outs_p = ([], [], [])
    outs_s = ([], [], [])
    for l in range(depth):
        last = l == depth - 1
        hp, pa, pq, ps = _mixer_stream(xp, zero_a, zero_qkv, zero_s, weights, l)
        xp = _mlp(hp, g2, wu, wd, gf, l, last)
        hs, sa, sq, ss = _mixer_step(xs, state_conv_a, state_conv_qkv, state_delta, weights, l)
        xs = _mlp(hs, g2, wu, wd, gf, l, last)
        for acc, val in zip(outs_p, (pa, pq, ps)):
            acc.append(val)
        for acc, val in zip(outs_s, (sa, sq, ss)):
            acc.append(val)

    return (xp, xs,
            jnp.stack(outs_p[0]), jnp.stack(outs_p[1]), jnp.stack(outs_p[2]),
            jnp.stack(outs_s[0]), jnp.stack(outs_s[1]), jnp.stack(outs_s[2]))
```

```python
import functools

import jax
import jax.numpy as jnp
from jax import lax
from jax.experimental import pallas as pl
from jax.experimental.pallas import tpu as pltpu

F32 = jnp.float32
BF16 = jnp.bfloat16

D_MODEL = 1024
W_A = 512
N_HEADS = 4
HEAD_K = 128
HEAD_V = 128
W_QK = N_HEADS * HEAD_K
W_VV = N_HEADS * HEAD_V
W_QKV = 2 * W_QK + W_VV
CONV_A_WIDTH = 3
CONV_QKV_WIDTH = 4
D_FF = 4 * D_MODEL
EPS = 1e-6

LANES = 128
SUBLANES = 8
BLOCK_ROWS = 128

OFF_A = 0
OFF_QKV = OFF_A + 3 * W_A
OFF_Z = OFF_QKV + W_QKV
OFF_G = OFF_Z + W_VV
GATE_SHIFT = 2 * N_HEADS
IN_WIDTH = OFF_G + GATE_SHIFT + 2 * D_MODEL

PROMPT_CHUNK = 64
MIXER_TILE = 512
MLP_TILE = 512
FF_CHUNK = 1024
PIECE = 256
VMEM_LIMIT = 56 * 1024 * 1024


def _mm(a, b):
    return jnp.dot(a.astype(BF16), b.astype(BF16), preferred_element_type=F32)


def _mm_nt(a, b):
    return lax.dot_general(a.astype(BF16), b.astype(BF16), (((1,), (1,)), ((), ())),
                           preferred_element_type=F32)


def _mm_tn(a, b):
    return lax.dot_general(a.astype(BF16), b.astype(BF16), (((0,), (0,)), ((), ())),
                           preferred_element_type=F32)


def _rms(x, g):
    return x * lax.rsqrt(jnp.mean(x * x, axis=-1, keepdims=True) + EPS) * g


def _sigmoid(x):
    return 1.0 / (1.0 + jnp.exp(-x))


def _softplus(x):
    return jnp.maximum(x, 0.0) + jnp.log1p(jnp.exp(-jnp.abs(x)))


def _chunk_scan(x, pos, chunk, reverse):
    rows = x.shape[0]
    s = 1
    while s < chunk:
        if reverse:
            x = x + jnp.where(pos < chunk - s, pltpu.roll(x, rows - s, 0), 0.0)
        else:
            x = x + jnp.where(pos >= s, pltpu.roll(x, s, 0), 0.0)
        s *= 2
    return x


def _mixer_kernel(x_ref, g1_ref, win_ref, wtail_ref, caw_ref, waout_ref, cqw_ref, alog_ref, dtb_ref,
                  og_ref, wbout_ref, wo_ref, sa_ref, sq_ref, s0_ref,
                  h_ref, na_ref, nq_ref, ns_ref,
                  pad_a, pad_q, o_scr, *, tm, chunk, streaming):
    nseq = tm // chunk
    row = lax.broadcasted_iota(jnp.int32, (tm, LANES), 0)
    pos = row & (chunk - 1)

    if streaming:
        first = pl.program_id(1) == 0

        @pl.when(first)
        def _():
            ns_ref[...] = s0_ref[...]
            pad_a[SUBLANES - (CONV_A_WIDTH - 1):SUBLANES, :] = sa_ref[...]
            pad_q[SUBLANES - (CONV_QKV_WIDTH - 1):SUBLANES, :] = sq_ref[...]

    fillers = []

    def fill(n=1):
        for _ in range(min(n, len(fillers))):
            fillers.pop(0)()

    def queue_projection(lhs, w_ref, col0, width, out):
        def piece(j):
            def run():
                out[j] = _mm(lhs(), w_ref[:, col0 + j * PIECE:col0 + (j + 1) * PIECE])
            return run
        fillers.extend(piece(j) for j in range(width // PIECE))

    def causal_conv(pad, xin, w_ref, width, state_ref, new_ref, act, fills):
        hw = width - 1
        taps = [w_ref[j:j + 1, :] for j in range(width)]
        if streaming:
            pad[SUBLANES:SUBLANES + tm, :] = xin
            outs = []
            for r0 in range(0, tm, BLOCK_ROWS):
                fill(fills)
                y = xin[r0:r0 + BLOCK_ROWS] * taps[hw]
                for s in range(1, width):
                    y = y + pad[SUBLANES - s + r0:SUBLANES - s + r0 + BLOCK_ROWS, :] * taps[hw - s]
                outs.append(act(y))
            new_ref[...] = pad[SUBLANES + tm - hw:SUBLANES + tm, :]
            pad[0:SUBLANES, :] = pad[tm:tm + SUBLANES, :]
            return jnp.concatenate(outs, axis=0)
        pad[0:tm, :] = xin
        pad[tm:2 * tm, :] = jnp.zeros((tm, xin.shape[1]), F32)
        for q in range(nseq):
            qp = (q - 1) % nseq
            pad[tm + qp * chunk + chunk - hw:tm + qp * chunk + chunk, :] = state_ref[q]
            new_ref[q] = pad[q * chunk + chunk - hw:q * chunk + chunk, :]
        hist = pad[tm:2 * tm, :]
        cpos = lax.broadcasted_iota(jnp.int32, (tm, 1), 0) & (chunk - 1)
        fill(fills)
        y = xin * taps[hw]
        for s in range(1, width):
            sh = jnp.where(cpos >= s, pltpu.roll(xin, s, 0), pltpu.roll(hist, s, 0))
            y = y + sh * taps[hw - s]
        return act(y)

    def wrap_lanes(piece, src):
        lane = lax.broadcasted_iota(jnp.int32, (tm, LANES), 1)
        return jnp.concatenate(
            [jnp.where(lane < GATE_SHIFT, src, piece[:, 0:LANES]), piece[:, LANES:]], axis=1)

    x = x_ref[...]
    xn = _rms(x, g1_ref[...]).astype(BF16)

    pa_cols = [None] * (3 * W_A // PIECE)
    z_cols = [None] * (W_VV // PIECE)
    gate_cols = [None] * (2 * D_MODEL // PIECE)
    ya_cols = [None] * (D_MODEL // PIECE)
    queue_projection(lambda: xn, win_ref, OFF_A, 3 * W_A, pa_cols)
    queue_projection(lambda: xn, win_ref, OFF_Z, W_VV, z_cols)
    queue_projection(lambda: xn, win_ref, OFF_G, 2 * D_MODEL, gate_cols)

    p_q = _mm(xn, win_ref[:, OFF_QKV:OFF_QKV + W_QKV])
    ba_raw = _mm(xn, win_ref[:, OFF_G:OFF_G + LANES])
    gate_tail = _mm(xn, wtail_ref[...])
    beta = _sigmoid(ba_raw)
    c = causal_conv(pad_q, p_q, cqw_ref, CONV_QKV_WIDTH, sq_ref, nq_ref, lambda y: y * _sigmoid(y), 0)
    fill(8)

    g = -jnp.exp(alog_ref[...]) * _softplus(ba_raw + dtb_ref[...])
    gc = _chunk_scan(g, pos, chunk, reverse=False)
    g_after = _chunk_scan(g, pos, chunk, reverse=True) - g
    eg = jnp.exp(gc)
    e_after = jnp.exp(g_after)
    gc_t = gc.T

    ri = lax.broadcasted_iota(jnp.int32, (BLOCK_ROWS, BLOCK_ROWS), 0)
    ci = lax.broadcasted_iota(jnp.int32, (BLOCK_ROWS, BLOCK_ROWS), 1)
    m_incl = ((ri & -chunk) == (ci & -chunk)) & (ri >= ci)
    eye = jnp.where(ri == ci, 1.0, 0.0)
    m_levels = []
    size = 1
    while size < chunk:
        m_levels.append(((ri & -(2 * size)) == (ci & -(2 * size))) & ((ri & -size) > (ci & -size)))
        size *= 2

    pairs = [(rb, h) for rb in range(tm // BLOCK_ROWS) for h in range(N_HEADS)]
    q_g, k_t, rhs, p_blk, a_blk, t_inv = {}, {}, {}, {}, {}, {}
    for rb, h in pairs:
        rows = slice(rb * BLOCK_ROWS, (rb + 1) * BLOCK_ROWS)
        qh = c[rows, h * HEAD_K:(h + 1) * HEAD_K]
        kh = c[rows, W_QK + h * HEAD_K:W_QK + (h + 1) * HEAD_K]
        vh = c[rows, 2 * W_QK + h * HEAD_V:2 * W_QK + (h + 1) * HEAD_V]
        qh = qh * (lax.rsqrt(jnp.sum(qh * qh, axis=-1, keepdims=True) + EPS) * (HEAD_K ** -0.5))
        kh = kh * lax.rsqrt(jnp.sum(kh * kh, axis=-1, keepdims=True) + EPS)
        b_col = beta[rows, h:h + 1]
        hg = N_HEADS + h
        eg_col = eg[rows, hg:hg + 1]
        diff = gc[rows, hg:hg + 1] - gc_t[hg:hg + 1, rows]
        dec_incl = jnp.exp(jnp.where(m_incl, diff, -1e30))
        qk = _mm_nt(jnp.concatenate([qh, kh], axis=0), kh)
        p_blk[rb, h] = qk[0:BLOCK_ROWS] * dec_incl
        a_blk[rb, h] = qk[BLOCK_ROWS:] * dec_incl * b_col
        rhs[rb, h] = jnp.concatenate([vh * b_col, kh * (b_col * eg_col)], axis=1)
        q_g[rb, h] = qh * eg_col
        k_t[rb, h] = kh * e_after[rows, hg:hg + 1]
        t_inv[rb, h] = eye - jnp.where(m_levels[0], a_blk[rb, h], 0.0)

    fill(D_MODEL // PIECE + 1)

    p_a = jnp.concatenate(pa_cols, axis=1)
    gated_u = causal_conv(pad_a, p_a[:, W_A:2 * W_A] * p_a[:, 2 * W_A:3 * W_A], caw_ref,
                          CONV_A_WIDTH, sa_ref, na_ref, lambda y: y, 0) * p_a[:, 0:W_A]
    gated_u = gated_u.astype(BF16)
    queue_projection(lambda: gated_u, waout_ref, 0, D_MODEL, ya_cols)
    z = jnp.concatenate(z_cols, axis=1)
    z_act = z * _sigmoid(z)
    n_gp = D_MODEL // PIECE
    sig_a = _sigmoid(jnp.concatenate(
        [wrap_lanes(gate_cols[0], gate_cols[n_gp][:, 0:LANES])] + gate_cols[1:n_gp], axis=1))

    for m_off in m_levels[1:]:
        a_t = {p: _mm(jnp.where(m_off, a_blk[p], 0.0), t_inv[p]) for p in pairs}
        t_inv = {p: t_inv[p] - _mm(t_inv[p], a_t[p]) for p in pairs}
    w_vk = {p: _mm(t_inv[p], rhs[p]) for p in pairs}

    cpb = BLOCK_ROWS // chunk
    steps = [[(q, h) for h in range(N_HEADS)] for q in range(tm // chunk)] if streaming else \
        [[(q, h) for q in range(tm // chunk) for h in range(N_HEADS)]]
    for step in steps:
        s_old, xs, u_c = {}, {}, {}
        for q, h in step:
            p, c0 = (q // cpb, h), (q % cpb) * chunk
            s_old[q, h] = ns_ref[h] if streaming else s0_ref[q, h]
            xs[q, h] = _mm(jnp.concatenate([w_vk[p][c0:c0 + chunk, HEAD_V:], q_g[p][c0:c0 + chunk]],
                                           axis=0), s_old[q, h])
        fill()
        for q, h in step:
            p, c0 = (q // cpb, h), (q % cpb) * chunk
            u_c[q, h] = w_vk[p][c0:c0 + chunk, 0:HEAD_V] - xs[q, h][0:chunk]
            o_c = xs[q, h][chunk:] + _mm(p_blk[p][c0:c0 + chunk, c0:c0 + chunk], u_c[q, h])
            o_scr[q * chunk:(q + 1) * chunk, h * HEAD_V:(h + 1) * HEAD_V] = o_c
        for q, h in step:
            p, c0 = (q // cpb, h), (q % cpb) * chunk
            g_last = eg[(q + 1) * chunk - 1:(q + 1) * chunk, N_HEADS + h:N_HEADS + h + 1]
            s_new = s_old[q, h] * g_last + _mm_tn(k_t[p][c0:c0 + chunk], u_c[q, h])
            if streaming:
                ns_ref[h] = s_new
            else:
                ns_ref[q, h] = s_new
    fill(len(fillers))

    og = og_ref[...]
    o_parts = []
    for h in range(N_HEADS):
        o_h = o_scr[:, h * HEAD_V:(h + 1) * HEAD_V]
        o_parts.append(_rms(o_h, og) * z_act[:, h * HEAD_V:(h + 1) * HEAD_V])
    y_b = _mm(jnp.concatenate(o_parts, axis=1), wbout_ref[...])

    sig_b = _sigmoid(jnp.concatenate(
        [wrap_lanes(gate_cols[n_gp], gate_tail)] + gate_cols[n_gp + 1:], axis=1))
    mixed = sig_a * jnp.concatenate(ya_cols, axis=1) + sig_b * y_b
    h_ref[...] = x + _mm(mixed, wo_ref[...])


def _mlp_kernel(h_ref, g2_ref, wup_ref, wdown_ref, gf_ref, out_ref, *, final_norm):
    h = h_ref[...]
    hn = _rms(h, g2_ref[...]).astype(BF16)
    acc = h
    for j in range(D_FF // FF_CHUNK):
        up = jnp.dot(hn, wup_ref[:, j * FF_CHUNK:(j + 1) * FF_CHUNK], preferred_element_type=F32)
        act = jnp.maximum(up, 0.0)
        acc = acc + _mm(act * act, wdown_ref[j * FF_CHUNK:(j + 1) * FF_CHUNK, :])
    if final_norm:
        acc = _rms(acc, gf_ref[...])
    out_ref[...] = acc


def _resident(shape, layer):
    nd = len(shape)
    return pl.BlockSpec((None,) + shape, lambda *_: (layer,) + (0,) * nd,
                        pipeline_mode=pl.Buffered(1))


def _mixer_weight_specs(layer):
    return [
        _resident((1, D_MODEL), layer),
        _resident((D_MODEL, IN_WIDTH), layer),
        _resident((D_MODEL, LANES), layer),
        _resident((CONV_A_WIDTH, W_A), layer),
        _resident((W_A, D_MODEL), layer),
        _resident((CONV_QKV_WIDTH, W_QKV), layer),
        _resident((1, LANES), layer),
        _resident((1, LANES), layer),
        _resident((1, HEAD_V), layer),
        _resident((W_VV, D_MODEL), layer),
        _resident((D_MODEL, D_MODEL), layer),
    ]


def _mixer_stream(x, buf_a, buf_qkv, s0, weights, layer):
    nb, seq, _ = x.shape
    tm = MIXER_TILE
    kern = functools.partial(_mixer_kernel, tm=tm, chunk=PROMPT_CHUNK, streaming=True)
    per_b3 = lambda b, i: (b, 0, 0)
    per_b4 = lambda b, i: (b, 0, 0, 0)
    return pl.pallas_call(
        kern,
        grid=(nb, seq // tm),
        in_specs=[pl.BlockSpec((None, tm, D_MODEL), lambda b, i: (b, i, 0))]
        + _mixer_weight_specs(layer) + [
            pl.BlockSpec((None, CONV_A_WIDTH - 1, W_A), per_b3),
            pl.BlockSpec((None, CONV_QKV_WIDTH - 1, W_QKV), per_b3),
            pl.BlockSpec((None, N_HEADS, HEAD_K, HEAD_V), per_b4),
        ],
        out_specs=[
            pl.BlockSpec((None, tm, D_MODEL), lambda b, i: (b, i, 0)),
            pl.BlockSpec((None, CONV_A_WIDTH - 1, W_A), per_b3),
            pl.BlockSpec((None, CONV_QKV_WIDTH - 1, W_QKV), per_b3),
            pl.BlockSpec((None, N_HEADS, HEAD_K, HEAD_V), per_b4),
        ],
        out_shape=[
            jax.ShapeDtypeStruct(x.shape, F32),
            jax.ShapeDtypeStruct(buf_a.shape, F32),
            jax.ShapeDtypeStruct(buf_qkv.shape, F32),
            jax.ShapeDtypeStruct(s0.shape, F32),
        ],
        scratch_shapes=[
            pltpu.VMEM((SUBLANES + tm, W_A), F32),
            pltpu.VMEM((SUBLANES + tm, W_QKV), F32),
            pltpu.VMEM((tm, W_VV), F32),
        ],
        compiler_params=pltpu.CompilerParams(
            dimension_semantics=("arbitrary", "arbitrary"), vmem_limit_bytes=VMEM_LIMIT),
        name="mixer_stream",
    )(x, *weights, buf_a, buf_qkv, s0)


def _mixer_step(x, buf_a, buf_qkv, s0, weights, layer):
    nb, seq, _ = x.shape
    tm = BLOCK_ROWS
    nseq = tm // seq
    kern = functools.partial(_mixer_kernel, tm=tm, chunk=seq, streaming=False)
    grp3 = lambda i: (i, 0, 0)
    grp4 = lambda i: (i, 0, 0, 0)
    lay3 = lambda i: (layer, i, 0, 0)
    lay4 = lambda i: (layer, i, 0, 0, 0)
    x2 = x.reshape(nb * seq, D_MODEL)
    h2, na, nq, ns = pl.pallas_call(
        kern,
        grid=(nb // nseq,),
        in_specs=[pl.BlockSpec((tm, D_MODEL), lambda i: (i, 0))] + _mixer_weight_specs(layer) + [
            pl.BlockSpec((None, nseq, CONV_A_WIDTH - 1, W_A), lay3),
            pl.BlockSpec((None, nseq, CONV_QKV_WIDTH - 1, W_QKV), lay3),
            pl.BlockSpec((None, nseq, N_HEADS, HEAD_K, HEAD_V), lay4),
        ],
        out_specs=[
            pl.BlockSpec((tm, D_MODEL), lambda i: (i, 0)),
            pl.BlockSpec((nseq, CONV_A_WIDTH - 1, W_A), grp3),
            pl.BlockSpec((nseq, CONV_QKV_WIDTH - 1, W_QKV), grp3),
            pl.BlockSpec((nseq, N_HEADS, HEAD_K, HEAD_V), grp4),
        ],
        out_shape=[
            jax.ShapeDtypeStruct(x2.shape, F32),
            jax.ShapeDtypeStruct(buf_a.shape[1:], F32),
            jax.ShapeDtypeStruct(buf_qkv.shape[1:], F32),
            jax.ShapeDtypeStruct(s0.shape[1:], F32),
        ],
        scratch_shapes=[
            pltpu.VMEM((2 * tm, W_A), F32),
            pltpu.VMEM((2 * tm, W_QKV), F32),
            pltpu.VMEM((tm, W_VV), F32),
        ],
        compiler_params=pltpu.CompilerParams(
            dimension_semantics=("arbitrary",), vmem_limit_bytes=VMEM_LIMIT),
        name="mixer_step",
    )(x2, *weights, buf_a, buf_qkv, s0)
    return h2.reshape(x.shape), na, nq, ns


def _mlp(h, g2, w_up, w_down, gf, layer, final_norm):
    shape = h.shape
    h2 = h.reshape(-1, D_MODEL)
    rows = h2.shape[0]
    tm = min(MLP_TILE, rows)
    out = pl.pallas_call(
        functools.partial(_mlp_kernel, final_norm=final_norm),
        grid=(rows // tm,),
        in_specs=[
            pl.BlockSpec((tm, D_MODEL), lambda i: (i, 0)),
            _resident((1, D_MODEL), layer),
            _resident((D_MODEL, D_FF), layer),
            _resident((D_FF, D_MODEL), layer),
            _resident((1, D_MODEL), 0),
        ],
        out_specs=pl.BlockSpec((tm, D_MODEL), lambda i: (i, 0)),
        out_shape=jax.ShapeDtypeStruct(h2.shape, F32),
        compiler_params=pltpu.CompilerParams(
            dimension_semantics=("arbitrary",), vmem_limit_bytes=VMEM_LIMIT),
        name="mlp",
    )(h2, g2, w_up, w_down, gf)
    return out.reshape(shape)


def _head_rows(v):
    return jnp.pad(v.astype(F32), ((0, 0), (N_HEADS, LANES - 2 * N_HEADS)))[:, None, :]


def kernel(x_prompt, x_sample, state_conv_a, state_conv_qkv, state_delta, norm1_g, w_in, conv_a_w,
           w_a_out, conv_qkv_w, a_log, dt_bias, onorm_g, w_b_out, w_o, norm2_g, w_up, w_down,
           final_g):
    depth = w_in.shape[0]
    nb = x_prompt.shape[0]
    zero_a = jnp.zeros((nb, CONV_A_WIDTH - 1, W_A), F32)
    zero_qkv = jnp.zeros((nb, CONV_QKV_WIDTH - 1, W_QKV), F32)
    zero_s = jnp.zeros((nb, N_HEADS, HEAD_K, HEAD_V), F32)

    w_in_b = w_in.astype(BF16)
    w_tail = jnp.pad(w_in_b[:, :, IN_WIDTH - GATE_SHIFT:], ((0, 0), (0, 0), (0, LANES - GATE_SHIFT)))
    weights = (
        norm1_g[:, None, :], w_in_b, w_tail, conv_a_w,
        jnp.roll(w_a_out.astype(BF16), GATE_SHIFT, axis=2), conv_qkv_w,
        _head_rows(a_log), _head_rows(dt_bias), onorm_g[:, None, :],
        jnp.roll(w_b_out.astype(BF16), GATE_SHIFT, axis=2),
        jnp.roll(w_o.astype(BF16), GATE_SHIFT, axis=1))
    g2 = norm2_g[:, None, :]
    wu = w_up.astype(BF16)
    wd = w_down.astype(BF16)
    gf = final_g.reshape(1, 1, D_MODEL)

    xp, xs = x_prompt, x_sample
    outs_p = ([], [], [])
    outs_s = ([], [], [])
    for l in range(depth):
        last = l == depth - 1
        hp, pa, pq, ps = _mixer_stream(xp, zero_a, zero_qkv, zero_s, weights, l)
        xp = _mlp(hp, g2, wu, wd, gf, l, last)
        hs, sa, sq, ss = _mixer_step(xs, state_conv_a, state_conv_qkv, state_delta, weights, l)
        xs = _mlp(hs, g2, wu, wd, gf, l, last)
        for acc, val in zip(outs_p, (pa, pq, ps)):
            acc.append(val)
        for acc, val in zip(outs_s, (sa, sq, ss)):
            acc.append(val)

    return (xp, xs,
            jnp.stack(outs_p[0]), jnp.stack(outs_p[1]), jnp.stack(outs_p[2]),
            jnp.stack(outs_s[0]), jnp.stack(outs_s[1]), jnp.stack(outs_s[2]))
```

```python
import functools

import jax
import jax.numpy as jnp
from jax import lax
from jax.experimental import pallas as pl
from jax.experimental.pallas import tpu as pltpu

F32 = jnp.float32
BF16 = jnp.bfloat16

D_MODEL = 1024
W_A = 512
N_HEADS = 4
HEAD_K = 128
HEAD_V = 128
W_QK = N_HEADS * HEAD_K
W_VV = N_HEADS * HEAD_V
W_QKV = 2 * W_QK + W_VV
CONV_A_WIDTH = 3
CONV_QKV_WIDTH = 4
D_FF = 4 * D_MODEL
EPS = 1e-6

LANES = 128
SUBLANES = 8
BLOCK_ROWS = 128

OFF_A = 0
OFF_QKV = OFF_A + 3 * W_A
OFF_Z = OFF_QKV + W_QKV
OFF_G = OFF_Z + W_VV
GATE_SHIFT = 2 * N_HEADS
IN_WIDTH = OFF_G + GATE_SHIFT + 2 * D_MODEL

PROMPT_CHUNK = 64
MIXER_TILE = 256
MLP_TILE = 512
FF_CHUNK = 1024
FF_PIECE = 512
PIECE = 256
VMEM_LIMIT = 56 * 1024 * 1024


def _mm(a, b):
    return jnp.dot(a.astype(BF16), b.astype(BF16), preferred_element_type=F32)


def _mm_nt(a, b):
    return lax.dot_general(a.astype(BF16), b.astype(BF16), (((1,), (1,)), ((), ())),
                           preferred_element_type=F32)


def _mm_tn(a, b):
    return lax.dot_general(a.astype(BF16), b.astype(BF16), (((0,), (0,)), ((), ())),
                           preferred_element_type=F32)


def _rms(x, g):
    return x * lax.rsqrt(jnp.mean(x * x, axis=-1, keepdims=True) + EPS) * g


def _sigmoid(x):
    return 1.0 / (1.0 + jnp.exp(-x))


def _softplus(x):
    return jnp.maximum(x, 0.0) + jnp.log1p(jnp.exp(-jnp.abs(x)))


def _chunk_scan(x, pos, chunk, reverse):
    rows = x.shape[0]
    s = 1
    while s < chunk:
        if reverse:
            x = x + jnp.where(pos < chunk - s, pltpu.roll(x, rows - s, 0), 0.0)
        else:
            x = x + jnp.where(pos >= s, pltpu.roll(x, s, 0), 0.0)
        s *= 2
    return x


def _mixer_kernel(*refs, tm, chunk, streaming, tiles_per_seq=None, n_tiles=None, final_norm=False):
    (x_ref, g1_ref, win_ref, wtail_ref, caw_ref, waout_ref, cqw_ref, alog_ref, dtb_ref, og_ref,
     wbout_ref, wo_ref) = refs[:12]
    refs = refs[12:]
    if streaming:
        g2_ref, wup_ref, wdown_ref, gf_ref = refs[:4]
        refs = refs[4:]
    sa_ref, sq_ref, s0_ref, out_ref, na_ref, nq_ref, ns_ref, pad_a, pad_q, o_scr = refs[:10]
    if streaming:
        h_scr, hn_scr = refs[10:]

    nseq = tm // chunk
    row = lax.broadcasted_iota(jnp.int32, (tm, LANES), 0)
    pos = row & (chunk - 1)

    if streaming:
        step = pl.program_id(0)
        valid = step < n_tiles

        @pl.when(step == 0)
        def _():
            h_scr[...] = jnp.zeros(h_scr.shape, F32)
            hn_scr[...] = jnp.zeros(hn_scr.shape, BF16)

        @pl.when(jnp.logical_and(lax.rem(step, tiles_per_seq) == 0, valid))
        def _():
            ns_ref[...] = s0_ref[...]
            pad_a[SUBLANES - (CONV_A_WIDTH - 1):SUBLANES, :] = sa_ref[...]
            pad_q[SUBLANES - (CONV_QKV_WIDTH - 1):SUBLANES, :] = sq_ref[...]

    fillers = []

    def fill(n=1):
        for _ in range(min(n, len(fillers))):
            fillers.pop(0)()

    def queue_projection(lhs, w_ref, col0, width, out):
        def piece(j):
            def run():
                out[j] = _mm(lhs(), w_ref[:, col0 + j * PIECE:col0 + (j + 1) * PIECE])
            return run
        fillers.extend(piece(j) for j in range(width // PIECE))

    def causal_conv(pad, xin, w_ref, width, state_ref, new_ref, act, fills):
        hw = width - 1
        taps = [w_ref[j:j + 1, :] for j in range(width)]
        if streaming:
            pad[SUBLANES:SUBLANES + tm, :] = xin
            outs = []
            for r0 in range(0, tm, BLOCK_ROWS):
                fill(fills)
                y = xin[r0:r0 + BLOCK_ROWS] * taps[hw]
                for s in range(1, width):
                    y = y + pad[SUBLANES - s + r0:SUBLANES - s + r0 + BLOCK_ROWS, :] * taps[hw - s]
                outs.append(act(y))
            new_ref[...] = pad[SUBLANES + tm - hw:SUBLANES + tm, :]
            pad[0:SUBLANES, :] = pad[tm:tm + SUBLANES, :]
            return jnp.concatenate(outs, axis=0)
        pad[0:tm, :] = xin
        pad[tm:2 * tm, :] = jnp.zeros((tm, xin.shape[1]), F32)
        for q in range(nseq):
            qp = (q - 1) % nseq
            pad[tm + qp * chunk + chunk - hw:tm + qp * chunk + chunk, :] = state_ref[q]
            new_ref[q] = pad[q * chunk + chunk - hw:q * chunk + chunk, :]
        hist = pad[tm:2 * tm, :]
        cpos = lax.broadcasted_iota(jnp.int32, (tm, 1), 0) & (chunk - 1)
        fill(fills)
        y = xin * taps[hw]
        for s in range(1, width):
            sh = jnp.where(cpos >= s, pltpu.roll(xin, s, 0), pltpu.roll(hist, s, 0))
            y = y + sh * taps[hw - s]
        return act(y)

    def wrap_lanes(piece, src):
        lane = lax.broadcasted_iota(jnp.int32, (tm, LANES), 1)
        return jnp.concatenate(
            [jnp.where(lane < GATE_SHIFT, src, piece[:, 0:LANES]), piece[:, LANES:]], axis=1)

    x = x_ref[...]
    xn = _rms(x, g1_ref[...]).astype(BF16)

    mlp_queue = []

    def fill_mlp(n=1):
        for _ in range(min(n, len(mlp_queue))):
            mlp_queue.pop(0)()

    if streaming:
        hn_prev = hn_scr[...]
        mlp_acc = [h_scr[...]]
        n_ff = D_FF // FF_PIECE
        acts = [None] * n_ff

        def up_piece(j):
            def run():
                up = jnp.dot(hn_prev, wup_ref[:, j * FF_PIECE:(j + 1) * FF_PIECE],
                             preferred_element_type=F32)
                up = jnp.maximum(up, 0.0)
                acts[j] = (up * up).astype(BF16)
            return run

        def down_piece(j):
            def run():
                mlp_acc[0] = mlp_acc[0] + jnp.dot(acts[j], wdown_ref[j * FF_PIECE:(j + 1) * FF_PIECE, :],
                                                  preferred_element_type=F32)
            return run

        lead = n_ff // 2
        mlp_queue.extend(up_piece(j) for j in range(lead))
        for j in range(n_ff):
            mlp_queue.append(down_piece(j))
            if j + lead < n_ff:
                mlp_queue.append(up_piece(j + lead))
        fill_mlp(2)

    pa_cols = [None] * (3 * W_A // PIECE)
    z_cols = [None] * (W_VV // PIECE)
    gate_cols = [None] * (2 * D_MODEL // PIECE)
    ya_cols = [None] * (D_MODEL // PIECE)
    queue_projection(lambda: xn, win_ref, OFF_A, 3 * W_A, pa_cols)
    queue_projection(lambda: xn, win_ref, OFF_Z, W_VV, z_cols)
    queue_projection(lambda: xn, win_ref, OFF_G, 2 * D_MODEL, gate_cols)

    p_q = _mm(xn, win_ref[:, OFF_QKV:OFF_QKV + W_QKV])
    ba_raw = _mm(xn, win_ref[:, OFF_G:OFF_G + LANES])
    gate_tail = _mm(xn, wtail_ref[...])
    beta = _sigmoid(ba_raw)
    c = causal_conv(pad_q, p_q, cqw_ref, CONV_QKV_WIDTH, sq_ref, nq_ref, lambda y: y * _sigmoid(y), 0)
    fill(8)
    fill_mlp(2)

    g = -jnp.exp(alog_ref[...]) * _softplus(ba_raw + dtb_ref[...])
    gc = _chunk_scan(g, pos, chunk, reverse=False)
    g_after = _chunk_scan(g, pos, chunk, reverse=True) - g
    eg = jnp.exp(gc)
    e_after = jnp.exp(g_after)
    gc_t = gc.T
    fill_mlp()

    ri = lax.broadcasted_iota(jnp.int32, (BLOCK_ROWS, BLOCK_ROWS), 0)
    ci = lax.broadcasted_iota(jnp.int32, (BLOCK_ROWS, BLOCK_ROWS), 1)
    m_incl = ((ri & -chunk) == (ci & -chunk)) & (ri >= ci)
    eye = jnp.where(ri == ci, 1.0, 0.0)
    m_levels = []
    size = 1
    while size < chunk:
        m_levels.append(((ri & -(2 * size)) == (ci & -(2 * size))) & ((ri & -size) > (ci & -size)))
        size *= 2

    pairs = [(rb, h) for rb in range(tm // BLOCK_ROWS) for h in range(N_HEADS)]
    q_g, k_t, rhs, p_blk, a_blk, t_inv = {}, {}, {}, {}, {}, {}
    for rb, h in pairs:
        rows = slice(rb * BLOCK_ROWS, (rb + 1) * BLOCK_ROWS)
        qh = c[rows, h * HEAD_K:(h + 1) * HEAD_K]
        kh = c[rows, W_QK + h * HEAD_K:W_QK + (h + 1) * HEAD_K]
        vh = c[rows, 2 * W_QK + h * HEAD_V:2 * W_QK + (h + 1) * HEAD_V]
        qh = qh * (lax.rsqrt(jnp.sum(qh * qh, axis=-1, keepdims=True) + EPS) * (HEAD_K ** -0.5))
        kh = kh * lax.rsqrt(jnp.sum(kh * kh, axis=-1, keepdims=True) + EPS)
        b_col = beta[rows, h:h + 1]
        hg = N_HEADS + h
        eg_col = eg[rows, hg:hg + 1]
        diff = gc[rows, hg:hg + 1] - gc_t[hg:hg + 1, rows]
        dec_incl = jnp.exp(jnp.where(m_incl, diff, -1e30))
        qk = _mm_nt(jnp.concatenate([qh, kh], axis=0), kh)
        p_blk[rb, h] = qk[0:BLOCK_ROWS] * dec_incl
        a_blk[rb, h] = qk[BLOCK_ROWS:] * dec_incl * b_col
        rhs[rb, h] = jnp.concatenate([vh * b_col, kh * (b_col * eg_col)], axis=1)
        q_g[rb, h] = qh * eg_col
        k_t[rb, h] = kh * e_after[rows, hg:hg + 1]
        t_inv[rb, h] = eye - jnp.where(m_levels[0], a_blk[rb, h], 0.0)

    fill(D_MODEL // PIECE + 1)
    fill_mlp()

    p_a = jnp.concatenate(pa_cols, axis=1)
    gated_u = causal_conv(pad_a, p_a[:, W_A:2 * W_A] * p_a[:, 2 * W_A:3 * W_A], caw_ref,
                          CONV_A_WIDTH, sa_ref, na_ref, lambda y: y, 0) * p_a[:, 0:W_A]
    gated_u = gated_u.astype(BF16)
    queue_projection(lambda: gated_u, waout_ref, 0, D_MODEL, ya_cols)
    z = jnp.concatenate(z_cols, axis=1)
    z_act = z * _sigmoid(z)
    n_gp = D_MODEL // PIECE
    sig_a = _sigmoid(jnp.concatenate(
        [wrap_lanes(gate_cols[0], gate_cols[n_gp][:, 0:LANES])] + gate_cols[1:n_gp], axis=1))
    fill_mlp()

    for m_off in m_levels[1:]:
        a_t = {p: _mm(jnp.where(m_off, a_blk[p], 0.0), t_inv[p]) for p in pairs}
        t_inv = {p: t_inv[p] - _mm(t_inv[p], a_t[p]) for p in pairs}
    w_vk = {p: _mm(t_inv[p], rhs[p]) for p in pairs}
    fill_mlp(3)

    cpb = BLOCK_ROWS // chunk
    steps = [[(q, h) for h in range(N_HEADS)] for q in range(tm // chunk)] if streaming else \
        [[(q, h) for q in range(tm // chunk) for h in range(N_HEADS)]]
    for group in steps:
        s_old, xs, u_c = {}, {}, {}
        for q, h in group:
            p, c0 = (q // cpb, h), (q % cpb) * chunk
            s_old[q, h] = ns_ref[h] if streaming else s0_ref[q, h]
            xs[q, h] = _mm(jnp.concatenate([w_vk[p][c0:c0 + chunk, HEAD_V:], q_g[p][c0:c0 + chunk]],
                                           axis=0), s_old[q, h])
        fill()
        fill_mlp()
        for q, h in group:
            p, c0 = (q // cpb, h), (q % cpb) * chunk
            u_c[q, h] = w_vk[p][c0:c0 + chunk, 0:HEAD_V] - xs[q, h][0:chunk]
            o_c = xs[q, h][chunk:] + _mm(p_blk[p][c0:c0 + chunk, c0:c0 + chunk], u_c[q, h])
            o_scr[q * chunk:(q + 1) * chunk, h * HEAD_V:(h + 1) * HEAD_V] = o_c
        for q, h in group:
            p, c0 = (q // cpb, h), (q % cpb) * chunk
            g_last = eg[(q + 1) * chunk - 1:(q + 1) * chunk, N_HEADS + h:N_HEADS + h + 1]
            s_new = s_old[q, h] * g_last + _mm_tn(k_t[p][c0:c0 + chunk], u_c[q, h])
            if streaming:
                ns_ref[h] = jnp.where(valid, s_new, s_old[q, h])
            else:
                ns_ref[q, h] = s_new
    fill(len(fillers))

    og = og_ref[...]
    o_parts = []
    for h in range(N_HEADS):
        o_h = o_scr[:, h * HEAD_V:(h + 1) * HEAD_V]
        o_parts.append(_rms(o_h, og) * z_act[:, h * HEAD_V:(h + 1) * HEAD_V])
    y_b = _mm(jnp.concatenate(o_parts, axis=1), wbout_ref[...])

    sig_b = _sigmoid(jnp.concatenate(
        [wrap_lanes(gate_cols[n_gp], gate_tail)] + gate_cols[n_gp + 1:], axis=1))
    mixed = sig_a * jnp.concatenate(ya_cols, axis=1) + sig_b * y_b
    h_new = x + _mm(mixed, wo_ref[...])
    if streaming:
        h_scr[...] = h_new
        hn_scr[...] = _rms(h_new, g2_ref[...]).astype(BF16)
        fill_mlp(len(mlp_queue))
        y_prev = mlp_acc[0]
        out_ref[...] = _rms(y_prev, gf_ref[...]) if final_norm else y_prev
    else:
        out_ref[...] = h_new


def _mlp_kernel(h_ref, g2_ref, wup_ref, wdown_ref, gf_ref, out_ref, *, final_norm):
    h = h_ref[...]
    hn = _rms(h, g2_ref[...]).astype(BF16)
    acc = h
    for j in range(D_FF // FF_CHUNK):
        up = jnp.dot(hn, wup_ref[:, j * FF_CHUNK:(j + 1) * FF_CHUNK], preferred_element_type=F32)
        act = jnp.maximum(up, 0.0)
        acc = acc + _mm(act * act, wdown_ref[j * FF_CHUNK:(j + 1) * FF_CHUNK, :])
    if final_norm:
        acc = _rms(acc, gf_ref[...])
    out_ref[...] = acc


def _resident(shape, layer):
    nd = len(shape)
    return pl.BlockSpec((None,) + shape, lambda *_: (layer,) + (0,) * nd,
                        pipeline_mode=pl.Buffered(1))


def _mixer_weight_specs(layer):
    return [
        _resident((1, D_MODEL), layer),
        _resident((D_MODEL, IN_WIDTH), layer),
        _resident((D_MODEL, LANES), layer),
        _resident((CONV_A_WIDTH, W_A), layer),
        _resident((W_A, D_MODEL), layer),
        _resident((CONV_QKV_WIDTH, W_QKV), layer),
        _resident((1, LANES), layer),
        _resident((1, LANES), layer),
        _resident((1, HEAD_V), layer),
        _resident((W_VV, D_MODEL), layer),
        _resident((D_MODEL, D_MODEL), layer),
    ]


def _layer_stream(x, buf_a, buf_qkv, s0, weights, mlp_weights, layer, final_norm):
    nb, seq, _ = x.shape
    tm = MIXER_TILE
    tps = seq // tm
    n_tiles = nb * tps
    kern = functools.partial(_mixer_kernel, tm=tm, chunk=PROMPT_CHUNK, streaming=True,
                             tiles_per_seq=tps, n_tiles=n_tiles, final_norm=final_norm)
    mix_tile = lambda i: jnp.minimum(i, n_tiles - 1)
    mlp_tile = lambda i: jnp.maximum(i - 1, 0)
    per_b3 = lambda i: (mix_tile(i) // tps, 0, 0)
    per_b4 = lambda i: (mix_tile(i) // tps, 0, 0, 0)
    return pl.pallas_call(
        kern,
        grid=(n_tiles + 1,),
        in_specs=[pl.BlockSpec((None, tm, D_MODEL),
                               lambda i: (mix_tile(i) // tps, mix_tile(i) % tps, 0))]
        + _mixer_weight_specs(layer) + [
            _resident((1, D_MODEL), layer),
            _resident((D_MODEL, D_FF), layer),
            _resident((D_FF, D_MODEL), layer),
            _resident((1, D_MODEL), 0),
            pl.BlockSpec((None, CONV_A_WIDTH - 1, W_A), per_b3),
            pl.BlockSpec((None, CONV_QKV_WIDTH - 1, W_QKV), per_b3),
            pl.BlockSpec((None, N_HEADS, HEAD_K, HEAD_V), per_b4),
        ],
        out_specs=[
            pl.BlockSpec((None, tm, D_MODEL), lambda i: (mlp_tile(i) // tps, mlp_tile(i) % tps, 0)),
            pl.BlockSpec((None, CONV_A_WIDTH - 1, W_A), per_b3),
            pl.BlockSpec((None, CONV_QKV_WIDTH - 1, W_QKV), per_b3),
            pl.BlockSpec((None, N_HEADS, HEAD_K, HEAD_V), per_b4),
        ],
        out_shape=[
            jax.ShapeDtypeStruct(x.shape, F32),
            jax.ShapeDtypeStruct(buf_a.shape, F32),
            jax.ShapeDtypeStruct(buf_qkv.shape, F32),
            jax.ShapeDtypeStruct(s0.shape, F32),
        ],
        scratch_shapes=[
            pltpu.VMEM((SUBLANES + tm, W_A), F32),
            pltpu.VMEM((SUBLANES + tm, W_QKV), F32),
            pltpu.VMEM((tm, W_VV), F32),
            pltpu.VMEM((tm, D_MODEL), F32),
            pltpu.VMEM((tm, D_MODEL), BF16),
        ],
        compiler_params=pltpu.CompilerParams(
            dimension_semantics=("arbitrary",), vmem_limit_bytes=VMEM_LIMIT),
        name="layer_stream",
    )(x, *weights, *mlp_weights, buf_a, buf_qkv, s0)


def _mixer_step(x, buf_a, buf_qkv, s0, weights, layer):
    nb, seq, _ = x.shape
    tm = BLOCK_ROWS
    nseq = tm // seq
    kern = functools.partial(_mixer_kernel, tm=tm, chunk=seq, streaming=False)
    grp3 = lambda i: (i, 0, 0)
    grp4 = lambda i: (i, 0, 0, 0)
    lay3 = lambda i: (layer, i, 0, 0)
    lay4 = lambda i: (layer, i, 0, 0, 0)
    x2 = x.reshape(nb * seq, D_MODEL)
    h2, na, nq, ns = pl.pallas_call(
        kern,
        grid=(nb // nseq,),
        in_specs=[pl.BlockSpec((tm, D_MODEL), lambda i: (i, 0))] + _mixer_weight_specs(layer) + [
            pl.BlockSpec((None, nseq, CONV_A_WIDTH - 1, W_A), lay3),
            pl.BlockSpec((None, nseq, CONV_QKV_WIDTH - 1, W_QKV), lay3),
            pl.BlockSpec((None, nseq, N_HEADS, HEAD_K, HEAD_V), lay4),
        ],
        out_specs=[
            pl.BlockSpec((tm, D_MODEL), lambda i: (i, 0)),
            pl.BlockSpec((nseq, CONV_A_WIDTH - 1, W_A), grp3),
            pl.BlockSpec((nseq, CONV_QKV_WIDTH - 1, W_QKV), grp3),
            pl.BlockSpec((nseq, N_HEADS, HEAD_K, HEAD_V), grp4),
        ],
        out_shape=[
            jax.ShapeDtypeStruct(x2.shape, F32),
            jax.ShapeDtypeStruct(buf_a.shape[1:], F32),
            jax.ShapeDtypeStruct(buf_qkv.shape[1:], F32),
            jax.ShapeDtypeStruct(s0.shape[1:], F32),
        ],
        scratch_shapes=[
            pltpu.VMEM((2 * tm, W_A), F32),
            pltpu.VMEM((2 * tm, W_QKV), F32),
            pltpu.VMEM((tm, W_VV), F32),
        ],
        compiler_params=pltpu.CompilerParams(
            dimension_semantics=("arbitrary",), vmem_limit_bytes=VMEM_LIMIT),
        name="mixer_step",
    )(x2, *weights, buf_a, buf_qkv, s0)
    return h2.reshape(x.shape), na, nq, ns


def _mlp(h, g2, w_up, w_down, gf, layer, final_norm):
    shape = h.shape
    h2 = h.reshape(-1, D_MODEL)
    rows = h2.shape[0]
    tm = min(MLP_TILE, rows)
    out = pl.pallas_call(
        functools.partial(_mlp_kernel, final_norm=final_norm),
        grid=(rows // tm,),
        in_specs=[
            pl.BlockSpec((tm, D_MODEL), lambda i: (i, 0)),
            _resident((1, D_MODEL), layer),
            _resident((D_MODEL, D_FF), layer),
            _resident((D_FF, D_MODEL), layer),
            _resident((1, D_MODEL), 0),
        ],
        out_specs=pl.BlockSpec((tm, D_MODEL), lambda i: (i, 0)),
        out_shape=jax.ShapeDtypeStruct(h2.shape, F32),
        compiler_params=pltpu.CompilerParams(
            dimension_semantics=("arbitrary",), vmem_limit_bytes=VMEM_LIMIT),
        name="mlp",
    )(h2, g2, w_up, w_down, gf)
    return out.reshape(shape)


def _head_rows(v):
    return jnp.pad(v.astype(F32), ((0, 0), (N_HEADS, LANES - 2 * N_HEADS)))[:, None, :]


def kernel(x_prompt, x_sample, state_conv_a, state_conv_qkv, state_delta, norm1_g, w_in, conv_a_w,
           w_a_out, conv_qkv_w, a_log, dt_bias, onorm_g, w_b_out, w_o, norm2_g, w_up, w_down,
           final_g):
    depth = w_in.shape[0]
    nb = x_prompt.shape[0]
    zero_a = jnp.zeros((nb, CONV_A_WIDTH - 1, W_A), F32)
    zero_qkv = jnp.zeros((nb, CONV_QKV_WIDTH - 1, W_QKV), F32)
    zero_s = jnp.zeros((nb, N_HEADS, HEAD_K, HEAD_V), F32)

    w_in_b = w_in.astype(BF16)
    w_tail = jnp.pad(w_in_b[:, :, IN_WIDTH - GATE_SHIFT:], ((0, 0), (0, 0), (0, LANES - GATE_SHIFT)))
    weights = (
        norm1_g[:, None, :], w_in_b, w_tail, conv_a_w,
        jnp.roll(w_a_out.astype(BF16), GATE_SHIFT, axis=2), conv_qkv_w,
        _head_rows(a_log), _head_rows(dt_bias), onorm_g[:, None, :],
        jnp.roll(w_b_out.astype(BF16), GATE_SHIFT, axis=2),
        jnp.roll(w_o.astype(BF16), GATE_SHIFT, axis=1))
    g2 = norm2_g[:, None, :]
    wu = w_up.astype(BF16)
    wd = w_down.astype(BF16)
    gf = final_g.reshape(1, 1, D_MODEL)

    xp, xs = x_prompt, x_sample
    outs_p = ([], [], [])
    outs_s = ([], [], [])
    for l in range(depth):
        last = l == depth - 1
        xp, pa, pq, ps = _layer_stream(xp, zero_a, zero_qkv, zero_s, weights, (g2, wu, wd, gf), l, last)
        hs, sa, sq, ss = _mixer_step(xs, state_conv_a, state_conv_qkv, state_delta, weights, l)
        xs = _mlp(hs, g2, wu, wd, gf, l, last)
        for acc, val in zip(outs_p, (pa, pq, ps)):
            acc.append(val)
        for acc, val in zip(outs_s, (sa, sq, ss)):
            acc.append(val)

    return (xp, xs,
            jnp.stack(outs_p[0]), jnp.stack(outs_p[1]), jnp.stack(outs_p[2]),
            jnp.stack(outs_s[0]), jnp.stack(outs_s[1]), jnp.stack(outs_s[2]))
```

```python
import functools

import jax
import jax.numpy as jnp
from jax import lax
from jax.experimental import pallas as pl
from jax.experimental.pallas import tpu as pltpu

F32 = jnp.float32
BF16 = jnp.bfloat16

D_MODEL = 1024
W_A = 512
N_HEADS = 4
HEAD_K = 128
HEAD_V = 128
W_QK = N_HEADS * HEAD_K
W_VV = N_HEADS * HEAD_V
W_QKV = 2 * W_QK + W_VV
CONV_A_WIDTH = 3
CONV_QKV_WIDTH = 4
D_FF = 4 * D_MODEL
EPS = 1e-6

LANES = 128
SUBLANES = 8
BLOCK_ROWS = 128

OFF_A = 0
OFF_QKV = OFF_A + 3 * W_A
OFF_Z = OFF_QKV + W_QKV
OFF_G = OFF_Z + W_VV
GATE_SHIFT = 2 * N_HEADS
IN_WIDTH = OFF_G + GATE_SHIFT + 2 * D_MODEL

PROMPT_CHUNK = 64
MIXER_TILE = 512
MLP_TILE = 1024
FF_CHUNK = 1024
PIECE = 256
VMEM_LIMIT = 56 * 1024 * 1024


def _mm(a, b):
    return jnp.dot(a.astype(BF16), b.astype(BF16), preferred_element_type=F32)


def _mm_nt(a, b):
    return lax.dot_general(a.astype(BF16), b.astype(BF16), (((1,), (1,)), ((), ())),
                           preferred_element_type=F32)


def _mm_tn(a, b):
    return lax.dot_general(a.astype(BF16), b.astype(BF16), (((0,), (0,)), ((), ())),
                           preferred_element_type=F32)


def _rms(x, g):
    return x * lax.rsqrt(jnp.mean(x * x, axis=-1, keepdims=True) + EPS) * g


def _sigmoid(x):
    return 1.0 / (1.0 + jnp.exp(-x))


def _softplus(x):
    return jnp.maximum(x, 0.0) + jnp.log1p(jnp.exp(-jnp.abs(x)))


def _chunk_scan(x, pos, chunk, reverse):
    rows = x.shape[0]
    s = 1
    while s < chunk:
        if reverse:
            x = x + jnp.where(pos < chunk - s, pltpu.roll(x, rows - s, 0), 0.0)
        else:
            x = x + jnp.where(pos >= s, pltpu.roll(x, s, 0), 0.0)
        s *= 2
    return x


def _mixer_kernel(x_ref, g1_ref, win_ref, wtail_ref, caw_ref, waout_ref, cqw_ref, alog_ref, dtb_ref,
                  og_ref, wbout_ref, wo_ref, sa_ref, sq_ref, s0_ref,
                  h_ref, na_ref, nq_ref, ns_ref,
                  pad_a, pad_q, o_scr, *, tm, chunk, streams):
    streaming = streams > 0
    ts = tm // streams if streaming else tm
    nseq = tm // chunk
    row = lax.broadcasted_iota(jnp.int32, (tm, LANES), 0)
    pos = row & (chunk - 1)

    if streaming:
        @pl.when(pl.program_id(0) == 0)
        def _():
            ns_ref[...] = s0_ref[...]
            pad_a[:, SUBLANES - (CONV_A_WIDTH - 1):SUBLANES, :] = sa_ref[...]
            pad_q[:, SUBLANES - (CONV_QKV_WIDTH - 1):SUBLANES, :] = sq_ref[...]

    fillers = []

    def fill(n=1):
        for _ in range(min(n, len(fillers))):
            fillers.pop(0)()

    def queue_projection(lhs, w_ref, col0, width, out):
        def piece(j):
            def run():
                out[j] = _mm(lhs(), w_ref[:, col0 + j * PIECE:col0 + (j + 1) * PIECE])
            return run
        fillers.extend(piece(j) for j in range(width // PIECE))

    def causal_conv(pad, xin, w_ref, width, state_ref, new_ref, act):
        hw = width - 1
        taps = [w_ref[j:j + 1, :] for j in range(width)]
        if streaming:
            outs = []
            for b in range(streams):
                pad[b, SUBLANES:SUBLANES + ts, :] = xin[b * ts:(b + 1) * ts]
                for r0 in range(0, ts, BLOCK_ROWS):
                    y = xin[b * ts + r0:b * ts + r0 + BLOCK_ROWS] * taps[hw]
                    for s in range(1, width):
                        y = y + pad[b, SUBLANES - s + r0:SUBLANES - s + r0 + BLOCK_ROWS, :] * taps[hw - s]
                    outs.append(act(y))
                new_ref[b] = pad[b, SUBLANES + ts - hw:SUBLANES + ts, :]
                pad[b, 0:SUBLANES, :] = pad[b, ts:ts + SUBLANES, :]
            return jnp.concatenate(outs, axis=0)
        pad[0:tm, :] = xin
        pad[tm:2 * tm, :] = jnp.zeros((tm, xin.shape[1]), F32)
        for q in range(nseq):
            qp = (q - 1) % nseq
            pad[tm + qp * chunk + chunk - hw:tm + qp * chunk + chunk, :] = state_ref[q]
            new_ref[q] = pad[q * chunk + chunk - hw:q * chunk + chunk, :]
        hist = pad[tm:2 * tm, :]
        cpos = lax.broadcasted_iota(jnp.int32, (tm, 1), 0) & (chunk - 1)
        y = xin * taps[hw]
        for s in range(1, width):
            sh = jnp.where(cpos >= s, pltpu.roll(xin, s, 0), pltpu.roll(hist, s, 0))
            y = y + sh * taps[hw - s]
        return act(y)

    def wrap_lanes(piece, src):
        lane = lax.broadcasted_iota(jnp.int32, (tm, LANES), 1)
        return jnp.concatenate(
            [jnp.where(lane < GATE_SHIFT, src, piece[:, 0:LANES]), piece[:, LANES:]], axis=1)

    x = x_ref[...].reshape(tm, D_MODEL)
    xn = _rms(x, g1_ref[...]).astype(BF16)

    pa_cols = [None] * (3 * W_A // PIECE)
    z_cols = [None] * (W_VV // PIECE)
    gate_cols = [None] * (2 * D_MODEL // PIECE)
    ya_cols = [None] * (D_MODEL // PIECE)
    queue_projection(lambda: xn, win_ref, OFF_A, 3 * W_A, pa_cols)
    queue_projection(lambda: xn, win_ref, OFF_Z, W_VV, z_cols)
    queue_projection(lambda: xn, win_ref, OFF_G, 2 * D_MODEL, gate_cols)

    p_q = _mm(xn, win_ref[:, OFF_QKV:OFF_QKV + W_QKV])
    ba_raw = _mm(xn, win_ref[:, OFF_G:OFF_G + LANES])
    gate_tail = _mm(xn, wtail_ref[...])
    beta = _sigmoid(ba_raw)
    c = causal_conv(pad_q, p_q, cqw_ref, CONV_QKV_WIDTH, sq_ref, nq_ref, lambda y: y * _sigmoid(y))
    fill(len(pa_cols) + len(z_cols))

    g = -jnp.exp(alog_ref[...]) * _softplus(ba_raw + dtb_ref[...])
    gc = _chunk_scan(g, pos, chunk, reverse=False)
    g_after = _chunk_scan(g, pos, chunk, reverse=True) - g
    eg = jnp.exp(gc)
    e_after = jnp.exp(g_after)
    gc_t = gc.T

    ri = lax.broadcasted_iota(jnp.int32, (BLOCK_ROWS, BLOCK_ROWS), 0)
    ci = lax.broadcasted_iota(jnp.int32, (BLOCK_ROWS, BLOCK_ROWS), 1)
    m_incl = ((ri & -chunk) == (ci & -chunk)) & (ri >= ci)
    eye = jnp.where(ri == ci, 1.0, 0.0)
    m_levels = []
    size = 1
    while size < chunk:
        m_levels.append(((ri & -(2 * size)) == (ci & -(2 * size))) & ((ri & -size) > (ci & -size)))
        size *= 2

    pairs = [(rb, h) for rb in range(tm // BLOCK_ROWS) for h in range(N_HEADS)]
    q_g, k_t, rhs, p_blk, a_blk, t_inv = {}, {}, {}, {}, {}, {}
    for rb, h in pairs:
        rows = slice(rb * BLOCK_ROWS, (rb + 1) * BLOCK_ROWS)
        qh = c[rows, h * HEAD_K:(h + 1) * HEAD_K]
        kh = c[rows, W_QK + h * HEAD_K:W_QK + (h + 1) * HEAD_K]
        vh = c[rows, 2 * W_QK + h * HEAD_V:2 * W_QK + (h + 1) * HEAD_V]
        qh = qh * (lax.rsqrt(jnp.sum(qh * qh, axis=-1, keepdims=True) + EPS) * (HEAD_K ** -0.5))
        kh = kh * lax.rsqrt(jnp.sum(kh * kh, axis=-1, keepdims=True) + EPS)
        b_col = beta[rows, h:h + 1]
        hg = N_HEADS + h
        eg_col = eg[rows, hg:hg + 1]
        diff = gc[rows, hg:hg + 1] - gc_t[hg:hg + 1, rows]
        dec_incl = jnp.exp(jnp.where(m_incl, diff, -1e30))
        qk = _mm_nt(jnp.concatenate([qh, kh], axis=0), kh)
        p_blk[rb, h] = qk[0:BLOCK_ROWS] * dec_incl
        a_blk[rb, h] = qk[BLOCK_ROWS:] * dec_incl * b_col
        rhs[rb, h] = jnp.concatenate([vh * b_col, kh * (b_col * eg_col)], axis=1)
        q_g[rb, h] = qh * eg_col
        k_t[rb, h] = kh * e_after[rows, hg:hg + 1]
        t_inv[rb, h] = eye - jnp.where(m_levels[0], a_blk[rb, h], 0.0)

    n_gp = D_MODEL // PIECE
    fill(n_gp + 1)

    p_a = jnp.concatenate(pa_cols, axis=1)
    gated_u = causal_conv(pad_a, p_a[:, W_A:2 * W_A] * p_a[:, 2 * W_A:3 * W_A], caw_ref,
                          CONV_A_WIDTH, sa_ref, na_ref, lambda y: y) * p_a[:, 0:W_A]
    gated_u = gated_u.astype(BF16)
    queue_projection(lambda: gated_u, waout_ref, 0, D_MODEL, ya_cols)
    z = jnp.concatenate(z_cols, axis=1)
    z_act = z * _sigmoid(z)
    sig_a = _sigmoid(jnp.concatenate(
        [wrap_lanes(gate_cols[0], gate_cols[n_gp][:, 0:LANES])] + gate_cols[1:n_gp], axis=1))

    for m_off in m_levels[1:]:
        a_t = {p: _mm(jnp.where(m_off, a_blk[p], 0.0), t_inv[p]) for p in pairs}
        t_inv = {p: t_inv[p] - _mm(t_inv[p], a_t[p]) for p in pairs}
    w_vk = {p: _mm(t_inv[p], rhs[p]) for p in pairs}

    cpb = BLOCK_ROWS // chunk
    if streaming:
        cps = ts // chunk
        groups = [[(b * cps + j, h, b) for b in range(streams) for h in range(N_HEADS)]
                  for j in range(cps)]
    else:
        groups = [[(q, h, q) for q in range(nseq) for h in range(N_HEADS)]]
    for group in groups:
        s_old, xs, u_c = {}, {}, {}
        for q, h, slot in group:
            p, c0 = (q // cpb, h), (q % cpb) * chunk
            s_old[q, h] = ns_ref[slot, h] if streaming else s0_ref[slot, h]
            xs[q, h] = _mm(jnp.concatenate([w_vk[p][c0:c0 + chunk, HEAD_V:], q_g[p][c0:c0 + chunk]],
                                           axis=0), s_old[q, h])
        fill()
        for q, h, slot in group:
            p, c0 = (q // cpb, h), (q % cpb) * chunk
            u_c[q, h] = w_vk[p][c0:c0 + chunk, 0:HEAD_V] - xs[q, h][0:chunk]
            o_c = xs[q, h][chunk:] + _mm(p_blk[p][c0:c0 + chunk, c0:c0 + chunk], u_c[q, h])
            o_scr[q * chunk:(q + 1) * chunk, h * HEAD_V:(h + 1) * HEAD_V] = o_c
        for q, h, slot in group:
            p, c0 = (q // cpb, h), (q % cpb) * chunk
            g_last = eg[(q + 1) * chunk - 1:(q + 1) * chunk, N_HEADS + h:N_HEADS + h + 1]
            ns_ref[slot, h] = s_old[q, h] * g_last + _mm_tn(k_t[p][c0:c0 + chunk], u_c[q, h])
        fill()
    fill(len(fillers))

    og = og_ref[...]
    o_parts = []
    for h in range(N_HEADS):
        o_h = o_scr[:, h * HEAD_V:(h + 1) * HEAD_V]
        o_parts.append(_rms(o_h, og) * z_act[:, h * HEAD_V:(h + 1) * HEAD_V])
    y_b = _mm(jnp.concatenate(o_parts, axis=1), wbout_ref[...])

    sig_b = _sigmoid(jnp.concatenate(
        [wrap_lanes(gate_cols[n_gp], gate_tail)] + gate_cols[n_gp + 1:], axis=1))
    mixed = sig_a * jnp.concatenate(ya_cols, axis=1) + sig_b * y_b
    h_ref[...] = (x + _mm(mixed, wo_ref[...])).reshape(h_ref.shape)


def _mlp_kernel(h_ref, g2_ref, wup_ref, wdown_ref, gf_ref, out_ref, *, final_norm):
    h = h_ref[...]
    hn = _rms(h, g2_ref[...]).astype(BF16)
    acc = h
    for j in range(D_FF // FF_CHUNK):
        up = jnp.dot(hn, wup_ref[:, j * FF_CHUNK:(j + 1) * FF_CHUNK], preferred_element_type=F32)
        act = jnp.maximum(up, 0.0)
        acc = acc + _mm(act * act, wdown_ref[j * FF_CHUNK:(j + 1) * FF_CHUNK, :])
    if final_norm:
        acc = _rms(acc, gf_ref[...])
    out_ref[...] = acc


def _resident(shape, layer):
    nd = len(shape)
    return pl.BlockSpec((None,) + shape, lambda *_: (layer,) + (0,) * nd,
                        pipeline_mode=pl.Buffered(1))


def _mixer_weight_specs(layer):
    return [
        _resident((1, D_MODEL), layer),
        _resident((D_MODEL, IN_WIDTH), layer),
        _resident((D_MODEL, LANES), layer),
        _resident((CONV_A_WIDTH, W_A), layer),
        _resident((W_A, D_MODEL), layer),
        _resident((CONV_QKV_WIDTH, W_QKV), layer),
        _resident((1, LANES), layer),
        _resident((1, LANES), layer),
        _resident((1, HEAD_V), layer),
        _resident((W_VV, D_MODEL), layer),
        _resident((D_MODEL, D_MODEL), layer),
    ]


def _mixer_stream(x, buf_a, buf_qkv, s0, weights, layer):
    nb, seq, _ = x.shape
    tm = MIXER_TILE
    ts = tm // nb
    kern = functools.partial(_mixer_kernel, tm=tm, chunk=PROMPT_CHUNK, streams=nb)
    whole3 = lambda i: (0, 0, 0)
    whole4 = lambda i: (0, 0, 0, 0)
    state_specs = [
        pl.BlockSpec((nb, CONV_A_WIDTH - 1, W_A), whole3),
        pl.BlockSpec((nb, CONV_QKV_WIDTH - 1, W_QKV), whole3),
        pl.BlockSpec((nb, N_HEADS, HEAD_K, HEAD_V), whole4),
    ]
    return pl.pallas_call(
        kern,
        grid=(seq // ts,),
        in_specs=[pl.BlockSpec((nb, ts, D_MODEL), lambda i: (0, i, 0))]
        + _mixer_weight_specs(layer) + state_specs,
        out_specs=[pl.BlockSpec((nb, ts, D_MODEL), lambda i: (0, i, 0))] + state_specs,
        out_shape=[
            jax.ShapeDtypeStruct(x.shape, F32),
            jax.ShapeDtypeStruct(buf_a.shape, F32),
            jax.ShapeDtypeStruct(buf_qkv.shape, F32),
            jax.ShapeDtypeStruct(s0.shape, F32),
        ],
        scratch_shapes=[
            pltpu.VMEM((nb, SUBLANES + ts, W_A), F32),
            pltpu.VMEM((nb, SUBLANES + ts, W_QKV), F32),
            pltpu.VMEM((tm, W_VV), F32),
        ],
        compiler_params=pltpu.CompilerParams(
            dimension_semantics=("arbitrary",), vmem_limit_bytes=VMEM_LIMIT),
        name="mixer_stream",
    )(x, *weights, buf_a, buf_qkv, s0)


def _mixer_step(x, buf_a, buf_qkv, s0, weights, layer):
    nb, seq, _ = x.shape
    tm = BLOCK_ROWS
    nseq = tm // seq
    kern = functools.partial(_mixer_kernel, tm=tm, chunk=seq, streams=0)
    grp3 = lambda i: (i, 0, 0)
    grp4 = lambda i: (i, 0, 0, 0)
    lay3 = lambda i: (layer, i, 0, 0)
    lay4 = lambda i: (layer, i, 0, 0, 0)
    x2 = x.reshape(nb * seq, D_MODEL)
    h2, na, nq, ns = pl.pallas_call(
        kern,
        grid=(nb // nseq,),
        in_specs=[pl.BlockSpec((tm, D_MODEL), lambda i: (i, 0))] + _mixer_weight_specs(layer) + [
            pl.BlockSpec((None, nseq, CONV_A_WIDTH - 1, W_A), lay3),
            pl.BlockSpec((None, nseq, CONV_QKV_WIDTH - 1, W_QKV), lay3),
            pl.BlockSpec((None, nseq, N_HEADS, HEAD_K, HEAD_V), lay4),
        ],
        out_specs=[
            pl.BlockSpec((tm, D_MODEL), lambda i: (i, 0)),
            pl.BlockSpec((nseq, CONV_A_WIDTH - 1, W_A), grp3),
            pl.BlockSpec((nseq, CONV_QKV_WIDTH - 1, W_QKV), grp3),
            pl.BlockSpec((nseq, N_HEADS, HEAD_K, HEAD_V), grp4),
        ],
        out_shape=[
            jax.ShapeDtypeStruct(x2.shape, F32),
            jax.ShapeDtypeStruct(buf_a.shape[1:], F32),
            jax.ShapeDtypeStruct(buf_qkv.shape[1:], F32),
            jax.ShapeDtypeStruct(s0.shape[1:], F32),
        ],
        scratch_shapes=[
            pltpu.VMEM((2 * tm, W_A), F32),
            pltpu.VMEM((2 * tm, W_QKV), F32),
            pltpu.VMEM((tm, W_VV), F32),
        ],
        compiler_params=pltpu.CompilerParams(
            dimension_semantics=("arbitrary",), vmem_limit_bytes=VMEM_LIMIT),
        name="mixer_step",
    )(x2, *weights, buf_a, buf_qkv, s0)
    return h2.reshape(x.shape), na, nq, ns


def _mlp(h, g2, w_up, w_down, gf, layer, final_norm):
    shape = h.shape
    h2 = h.reshape(-1, D_MODEL)
    rows = h2.shape[0]
    tm = min(MLP_TILE, rows)
    out = pl.pallas_call(
        functools.partial(_mlp_kernel, final_norm=final_norm),
        grid=(rows // tm,),
        in_specs=[
            pl.BlockSpec((tm, D_MODEL), lambda i: (i, 0)),
            _resident((1, D_MODEL), layer),
            _resident((D_MODEL, D_FF), layer),
            _resident((D_FF, D_MODEL), layer),
            _resident((1, D_MODEL), 0),
        ],
        out_specs=pl.BlockSpec((tm, D_MODEL), lambda i: (i, 0)),
        out_shape=jax.ShapeDtypeStruct(h2.shape, F32),
        compiler_params=pltpu.CompilerParams(
            dimension_semantics=("arbitrary",), vmem_limit_bytes=VMEM_LIMIT),
        name="mlp",
    )(h2, g2, w_up, w_down, gf)
    return out.reshape(shape)


def _head_rows(v):
    return jnp.pad(v.astype(F32), ((0, 0), (N_HEADS, LANES - 2 * N_HEADS)))[:, None, :]


def kernel(x_prompt, x_sample, state_conv_a, state_conv_qkv, state_delta, norm1_g, w_in, conv_a_w,
           w_a_out, conv_qkv_w, a_log, dt_bias, onorm_g, w_b_out, w_o, norm2_g, w_up, w_down,
           final_g):
    depth = w_in.shape[0]
    nb = x_prompt.shape[0]
    zero_a = jnp.zeros((nb, CONV_A_WIDTH - 1, W_A), F32)
    zero_qkv = jnp.zeros((nb, CONV_QKV_WIDTH - 1, W_QKV), F32)
    zero_s = jnp.zeros((nb, N_HEADS, HEAD_K, HEAD_V), F32)

    w_in_b = w_in.astype(BF16)
    w_tail = jnp.pad(w_in_b[:, :, IN_WIDTH - GATE_SHIFT:], ((0, 0), (0, 0), (0, LANES - GATE_SHIFT)))
    weights = (
        norm1_g[:, None, :], w_in_b, w_tail, conv_a_w,
        jnp.roll(w_a_out.astype(BF16), GATE_SHIFT, axis=2), conv_qkv_w,
        _head_rows(a_log), _head_rows(dt_bias), onorm_g[:, None, :],
        jnp.roll(w_b_out.astype(BF16), GATE_SHIFT, axis=2),
        jnp.roll(w_o.astype(BF16), GATE_SHIFT, axis=1))
    g2 = norm2_g[:, None, :]
    wu = w_up.astype(BF16)
    wd = w_down.astype(BF16)
    gf = final_g.reshape(1, 1, D_MODEL)

    xp, xs = x_prompt, x_sample
    outs_p = ([], [], [])
    outs_s = ([], [], [])
    for l in range(depth):
        last = l == depth - 1
        hp, pa, pq, ps = _mixer_stream(xp, zero_a, zero_qkv, zero_s, weights, l)
        xp = _mlp(hp, g2, wu, wd, gf, l, last)
        hs, sa, sq, ss = _mixer_step(xs, state_conv_a, state_conv_qkv, state_delta, weights, l)
        xs = _mlp(hs, g2, wu, wd, gf, l, last)
        for acc, val in zip(outs_p, (pa, pq, ps)):
            acc.append(val)
        for acc, val in zip(outs_s, (sa, sq, ss)):
            acc.append(val)

    return (xp, xs,
            jnp.stack(outs_p[0]), jnp.stack(outs_p[1]), jnp.stack(outs_p[2]),
            jnp.stack(outs_s[0]), jnp.stack(outs_s[1]), jnp.stack(outs_s[2]))
```

```python
import functools

import jax
import jax.numpy as jnp
from jax import lax
from jax.experimental import pallas as pl
from jax.experimental.pallas import tpu as pltpu

F32 = jnp.float32
BF16 = jnp.bfloat16

D_MODEL = 1024
W_A = 512
N_HEADS = 4
HEAD_K = 128
HEAD_V = 128
W_QK = N_HEADS * HEAD_K
W_VV = N_HEADS * HEAD_V
W_QKV = 2 * W_QK + W_VV
CONV_A_WIDTH = 3
CONV_QKV_WIDTH = 4
D_FF = 4 * D_MODEL
EPS = 1e-6

LANES = 128
SUBLANES = 8
BLOCK_ROWS = 128

OFF_A = 0
OFF_QKV = OFF_A + 3 * W_A
OFF_Z = OFF_QKV + W_QKV
OFF_G = OFF_Z + W_VV
GATE_SHIFT = 2 * N_HEADS
IN_WIDTH = OFF_G + GATE_SHIFT + 2 * D_MODEL

PROMPT_CHUNK = 64
MIXER_TILE = 512
MLP_TILE = 1024
FF_CHUNK = 1024
PIECE = 256
VMEM_LIMIT = 56 * 1024 * 1024


def _mm(a, b):
    return jnp.dot(a.astype(BF16), b.astype(BF16), preferred_element_type=F32)


def _mm_nt(a, b):
    return lax.dot_general(a.astype(BF16), b.astype(BF16), (((1,), (1,)), ((), ())),
                           preferred_element_type=F32)


def _mm_tn(a, b):
    return lax.dot_general(a.astype(BF16), b.astype(BF16), (((0,), (0,)), ((), ())),
                           preferred_element_type=F32)


def _rms(x, g):
    return x * lax.rsqrt(jnp.mean(x * x, axis=-1, keepdims=True) + EPS) * g


def _sigmoid(x):
    return 1.0 / (1.0 + jnp.exp(-x))


def _softplus(x):
    return jnp.maximum(x, 0.0) + jnp.log1p(jnp.exp(-jnp.abs(x)))


def _chunk_scan(x, pos, chunk, reverse):
    rows = x.shape[0]
    s = 1
    while s < chunk:
        if reverse:
            x = x + jnp.where(pos < chunk - s, pltpu.roll(x, rows - s, 0), 0.0)
        else:
            x = x + jnp.where(pos >= s, pltpu.roll(x, s, 0), 0.0)
        s *= 2
    return x


def _mixer_kernel(*refs, tm, chunk, streams):
    streaming = streams > 0
    (x_ref, g1_ref, win_ref, wsmall_ref, caw_ref, waout_ref, cqw_ref, alog_ref, dtb_ref, og_ref,
     wbout_ref, wo_ref, sa_ref, sq_ref, s0_ref) = refs[:15]
    refs = refs[15:]
    if streaming:
        xnext_ref, h_ref, na_ref, nq_ref, ns_ref, pad_a, pad_q, o_scr, xn_scr = refs
    else:
        h_ref, na_ref, nq_ref, ns_ref, pad_a, pad_q, o_scr = refs
    ts = tm // streams if streaming else tm
    nseq = tm // chunk
    row = lax.broadcasted_iota(jnp.int32, (tm, LANES), 0)
    pos = row & (chunk - 1)

    if streaming:
        @pl.when(pl.program_id(0) == 0)
        def _():
            ns_ref[...] = s0_ref[...]
            xn_scr[...] = _rms(x_ref[...].reshape(tm, D_MODEL), g1_ref[...]).astype(BF16)
            pad_a[:, SUBLANES - (CONV_A_WIDTH - 1):SUBLANES, :] = sa_ref[...]
            pad_q[:, SUBLANES - (CONV_QKV_WIDTH - 1):SUBLANES, :] = sq_ref[...]

    fillers = []

    def fill(n=1):
        for _ in range(min(n, len(fillers))):
            fillers.pop(0)()

    def queue_projection(lhs, w_ref, col0, width, out):
        def piece(j):
            def run():
                out[j] = _mm(lhs(), w_ref[:, col0 + j * PIECE:col0 + (j + 1) * PIECE])
            return run
        fillers.extend(piece(j) for j in range(width // PIECE))

    def causal_conv(pad, xin, w_ref, width, state_ref, new_ref, act):
        hw = width - 1
        taps = [w_ref[j:j + 1, :] for j in range(width)]
        if streaming:
            outs = []
            for b in range(streams):
                pad[b, SUBLANES:SUBLANES + ts, :] = xin[b * ts:(b + 1) * ts]
                for r0 in range(0, ts, BLOCK_ROWS):
                    y = xin[b * ts + r0:b * ts + r0 + BLOCK_ROWS] * taps[hw]
                    for s in range(1, width):
                        y = y + pad[b, SUBLANES - s + r0:SUBLANES - s + r0 + BLOCK_ROWS, :] * taps[hw - s]
                    outs.append(act(y))
                new_ref[b] = pad[b, SUBLANES + ts - hw:SUBLANES + ts, :]
                pad[b, 0:SUBLANES, :] = pad[b, ts:ts + SUBLANES, :]
            return jnp.concatenate(outs, axis=0)
        pad[0:tm, :] = xin
        pad[tm:2 * tm, :] = jnp.zeros((tm, xin.shape[1]), F32)
        for q in range(nseq):
            qp = (q - 1) % nseq
            pad[tm + qp * chunk + chunk - hw:tm + qp * chunk + chunk, :] = state_ref[q]
            new_ref[q] = pad[q * chunk + chunk - hw:q * chunk + chunk, :]
        hist = pad[tm:2 * tm, :]
        cpos = lax.broadcasted_iota(jnp.int32, (tm, 1), 0) & (chunk - 1)
        y = xin * taps[hw]
        for s in range(1, width):
            sh = jnp.where(cpos >= s, pltpu.roll(xin, s, 0), pltpu.roll(hist, s, 0))
            y = y + sh * taps[hw - s]
        return act(y)

    def wrap_lanes(piece, src):
        lane = lax.broadcasted_iota(jnp.int32, (tm, LANES), 1)
        return jnp.concatenate(
            [jnp.where(lane < GATE_SHIFT, src, piece[:, 0:LANES]), piece[:, LANES:]], axis=1)

    x = x_ref[...].reshape(tm, D_MODEL)
    xn = xn_scr[...] if streaming else _rms(x, g1_ref[...]).astype(BF16)

    pa_cols = [None] * (3 * W_A // PIECE)
    z_cols = [None] * (W_VV // PIECE)
    gate_cols = [None] * (2 * D_MODEL // PIECE)
    ya_cols = [None] * (D_MODEL // PIECE)
    queue_projection(lambda: xn, win_ref, OFF_A, 3 * W_A, pa_cols)
    queue_projection(lambda: xn, win_ref, OFF_Z, W_VV, z_cols)
    queue_projection(lambda: xn, win_ref, OFF_G, 2 * D_MODEL, gate_cols)

    p_q = _mm(xn, win_ref[:, OFF_QKV:OFF_QKV + W_QKV])
    small = _mm(xn, wsmall_ref[...])
    ba_raw = small[:, 0:LANES]
    gate_tail = small[:, LANES:]
    beta = _sigmoid(ba_raw)
    c = causal_conv(pad_q, p_q, cqw_ref, CONV_QKV_WIDTH, sq_ref, nq_ref, lambda y: y * _sigmoid(y))
    fill(len(pa_cols) + len(z_cols))

    g = -jnp.exp(alog_ref[...]) * _softplus(ba_raw + dtb_ref[...])
    gc = _chunk_scan(g, pos, chunk, reverse=False)
    g_after = _chunk_scan(g, pos, chunk, reverse=True) - g
    eg = jnp.exp(gc)
    e_after = jnp.exp(g_after)
    gc_t = gc.T

    ri = lax.broadcasted_iota(jnp.int32, (BLOCK_ROWS, BLOCK_ROWS), 0)
    ci = lax.broadcasted_iota(jnp.int32, (BLOCK_ROWS, BLOCK_ROWS), 1)
    m_incl = ((ri & -chunk) == (ci & -chunk)) & (ri >= ci)
    eye = jnp.where(ri == ci, 1.0, 0.0)
    m_levels = []
    size = 1
    while size < chunk:
        m_levels.append(((ri & -(2 * size)) == (ci & -(2 * size))) & ((ri & -size) > (ci & -size)))
        size *= 2

    pairs = [(rb, h) for rb in range(tm // BLOCK_ROWS) for h in range(N_HEADS)]
    q_g, k_t, rhs, p_blk, a_blk, t_inv = {}, {}, {}, {}, {}, {}
    for rb, h in pairs:
        rows = slice(rb * BLOCK_ROWS, (rb + 1) * BLOCK_ROWS)
        qh = c[rows, h * HEAD_K:(h + 1) * HEAD_K]
        kh = c[rows, W_QK + h * HEAD_K:W_QK + (h + 1) * HEAD_K]
        vh = c[rows, 2 * W_QK + h * HEAD_V:2 * W_QK + (h + 1) * HEAD_V]
        qh = qh * (lax.rsqrt(jnp.sum(qh * qh, axis=-1, keepdims=True) + EPS) * (HEAD_K ** -0.5))
        kh = kh * lax.rsqrt(jnp.sum(kh * kh, axis=-1, keepdims=True) + EPS)
        b_col = beta[rows, h:h + 1]
        hg = N_HEADS + h
        eg_col = eg[rows, hg:hg + 1]
        diff = gc[rows, hg:hg + 1] - gc_t[hg:hg + 1, rows]
        dec_incl = jnp.exp(jnp.where(m_incl, diff, -1e30))
        qk = _mm_nt(jnp.concatenate([qh, kh], axis=0), kh)
        p_blk[rb, h] = qk[0:BLOCK_ROWS] * dec_incl
        a_blk[rb, h] = qk[BLOCK_ROWS:] * dec_incl * b_col
        rhs[rb, h] = jnp.concatenate([vh * b_col, kh * (b_col * eg_col)], axis=1)
        q_g[rb, h] = qh * eg_col
        k_t[rb, h] = kh * e_after[rows, hg:hg + 1]
        t_inv[rb, h] = eye - jnp.where(m_levels[0], a_blk[rb, h], 0.0)

    n_gp = D_MODEL // PIECE
    fill(n_gp + 1)

    p_a = jnp.concatenate(pa_cols, axis=1)
    gated_u = causal_conv(pad_a, p_a[:, W_A:2 * W_A] * p_a[:, 2 * W_A:3 * W_A], caw_ref,
                          CONV_A_WIDTH, sa_ref, na_ref, lambda y: y) * p_a[:, 0:W_A]
    gated_u = gated_u.astype(BF16)
    queue_projection(lambda: gated_u, waout_ref, 0, D_MODEL, ya_cols)
    z = jnp.concatenate(z_cols, axis=1)
    z_act = z * _sigmoid(z)
    sig_a = _sigmoid(jnp.concatenate(
        [wrap_lanes(gate_cols[0], gate_cols[n_gp][:, 0:LANES])] + gate_cols[1:n_gp], axis=1))

    for m_off in m_levels[1:]:
        a_t = {p: _mm(jnp.where(m_off, a_blk[p], 0.0), t_inv[p]) for p in pairs}
        t_inv = {p: t_inv[p] - _mm(t_inv[p], a_t[p]) for p in pairs}
    w_vk = {p: _mm(t_inv[p], rhs[p]) for p in pairs}

    cpb = BLOCK_ROWS // chunk
    if streaming:
        cps = ts // chunk
        groups = [[(b * cps + j, h, b) for b in range(streams) for h in range(N_HEADS)]
                  for j in range(cps)]
    else:
        groups = [[(q, h, q) for q in range(nseq) for h in range(N_HEADS)]]
    for group in groups:
        s_old, xs, u_c = {}, {}, {}
        for q, h, slot in group:
            p, c0 = (q // cpb, h), (q % cpb) * chunk
            s_old[q, h] = ns_ref[slot, h] if streaming else s0_ref[slot, h]
            xs[q, h] = _mm(jnp.concatenate([w_vk[p][c0:c0 + chunk, HEAD_V:], q_g[p][c0:c0 + chunk]],
                                           axis=0), s_old[q, h])
        fill()
        for q, h, slot in group:
            p, c0 = (q // cpb, h), (q % cpb) * chunk
            u_c[q, h] = w_vk[p][c0:c0 + chunk, 0:HEAD_V] - xs[q, h][0:chunk]
            o_c = xs[q, h][chunk:] + _mm(p_blk[p][c0:c0 + chunk, c0:c0 + chunk], u_c[q, h])
            o_scr[q * chunk:(q + 1) * chunk, h * HEAD_V:(h + 1) * HEAD_V] = o_c
        for q, h, slot in group:
            p, c0 = (q // cpb, h), (q % cpb) * chunk
            g_last = eg[(q + 1) * chunk - 1:(q + 1) * chunk, N_HEADS + h:N_HEADS + h + 1]
            ns_ref[slot, h] = s_old[q, h] * g_last + _mm_tn(k_t[p][c0:c0 + chunk], u_c[q, h])
        fill()
    fill(len(fillers))

    og = og_ref[...]
    o_parts = []
    for h in range(N_HEADS):
        o_h = o_scr[:, h * HEAD_V:(h + 1) * HEAD_V]
        o_parts.append(_rms(o_h, og) * z_act[:, h * HEAD_V:(h + 1) * HEAD_V])
    y_b = _mm(jnp.concatenate(o_parts, axis=1), wbout_ref[...])

    sig_b = _sigmoid(jnp.concatenate(
        [wrap_lanes(gate_cols[n_gp], gate_tail)] + gate_cols[n_gp + 1:], axis=1))
    mixed = sig_a * jnp.concatenate(ya_cols, axis=1) + sig_b * y_b
    if streaming:
        xn_scr[...] = _rms(xnext_ref[...].reshape(tm, D_MODEL), g1_ref[...]).astype(BF16)
    h_ref[...] = (x + _mm(mixed, wo_ref[...])).reshape(h_ref.shape)


def _mlp_kernel(h_ref, g2_ref, wup_ref, wdown_ref, gf_ref, out_ref, *, final_norm):
    h = h_ref[...]
    hn = _rms(h, g2_ref[...]).astype(BF16)
    acc = h
    for j in range(D_FF // FF_CHUNK):
        up = jnp.dot(hn, wup_ref[:, j * FF_CHUNK:(j + 1) * FF_CHUNK], preferred_element_type=F32)
        act = jnp.maximum(up, 0.0)
        acc = acc + _mm(act * act, wdown_ref[j * FF_CHUNK:(j + 1) * FF_CHUNK, :])
    if final_norm:
        acc = _rms(acc, gf_ref[...])
    out_ref[...] = acc


def _resident(shape, layer):
    nd = len(shape)
    return pl.BlockSpec((None,) + shape, lambda *_: (layer,) + (0,) * nd,
                        pipeline_mode=pl.Buffered(1))


def _mixer_weight_specs(layer):
    return [
        _resident((1, D_MODEL), layer),
        _resident((D_MODEL, IN_WIDTH), layer),
        _resident((D_MODEL, 2 * LANES), layer),
        _resident((CONV_A_WIDTH, W_A), layer),
        _resident((W_A, D_MODEL), layer),
        _resident((CONV_QKV_WIDTH, W_QKV), layer),
        _resident((1, LANES), layer),
        _resident((1, LANES), layer),
        _resident((1, HEAD_V), layer),
        _resident((W_VV, D_MODEL), layer),
        _resident((D_MODEL, D_MODEL), layer),
    ]


def _mixer_stream(x, buf_a, buf_qkv, s0, weights, layer):
    nb, seq, _ = x.shape
    tm = MIXER_TILE
    ts = tm // nb
    n_steps = seq // ts
    kern = functools.partial(_mixer_kernel, tm=tm, chunk=PROMPT_CHUNK, streams=nb)
    whole3 = lambda i: (0, 0, 0)
    whole4 = lambda i: (0, 0, 0, 0)
    state_specs = [
        pl.BlockSpec((nb, CONV_A_WIDTH - 1, W_A), whole3),
        pl.BlockSpec((nb, CONV_QKV_WIDTH - 1, W_QKV), whole3),
        pl.BlockSpec((nb, N_HEADS, HEAD_K, HEAD_V), whole4),
    ]
    return pl.pallas_call(
        kern,
        grid=(n_steps,),
        in_specs=[pl.BlockSpec((nb, ts, D_MODEL), lambda i: (0, i, 0))]
        + _mixer_weight_specs(layer) + state_specs
        + [pl.BlockSpec((nb, ts, D_MODEL), lambda i: (0, jnp.minimum(i + 1, n_steps - 1), 0))],
        out_specs=[pl.BlockSpec((nb, ts, D_MODEL), lambda i: (0, i, 0))] + state_specs,
        out_shape=[
            jax.ShapeDtypeStruct(x.shape, F32),
            jax.ShapeDtypeStruct(buf_a.shape, F32),
            jax.ShapeDtypeStruct(buf_qkv.shape, F32),
            jax.ShapeDtypeStruct(s0.shape, F32),
        ],
        scratch_shapes=[
            pltpu.VMEM((nb, SUBLANES + ts, W_A), F32),
            pltpu.VMEM((nb, SUBLANES + ts, W_QKV), F32),
            pltpu.VMEM((tm, W_VV), F32),
            pltpu.VMEM((tm, D_MODEL), BF16),
        ],
        compiler_params=pltpu.CompilerParams(
            dimension_semantics=("arbitrary",), vmem_limit_bytes=VMEM_LIMIT),
        name="mixer_stream",
    )(x, *weights, buf_a, buf_qkv, s0, x)


def _mixer_step(x, buf_a, buf_qkv, s0, weights, layer):
    nb, seq, _ = x.shape
    tm = BLOCK_ROWS
    nseq = tm // seq
    kern = functools.partial(_mixer_kernel, tm=tm, chunk=seq, streams=0)
    grp3 = lambda i: (i, 0, 0)
    grp4 = lambda i: (i, 0, 0, 0)
    lay3 = lambda i: (layer, i, 0, 0)
    lay4 = lambda i: (layer, i, 0, 0, 0)
    x2 = x.reshape(nb * seq, D_MODEL)
    h2, na, nq, ns = pl.pallas_call(
        kern,
        grid=(nb // nseq,),
        in_specs=[pl.BlockSpec((tm, D_MODEL), lambda i: (i, 0))] + _mixer_weight_specs(layer) + [
            pl.BlockSpec((None, nseq, CONV_A_WIDTH - 1, W_A), lay3),
            pl.BlockSpec((None, nseq, CONV_QKV_WIDTH - 1, W_QKV), lay3),
            pl.BlockSpec((None, nseq, N_HEADS, HEAD_K, HEAD_V), lay4),
        ],
        out_specs=[
            pl.BlockSpec((tm, D_MODEL), lambda i: (i, 0)),
            pl.BlockSpec((nseq, CONV_A_WIDTH - 1, W_A), grp3),
            pl.BlockSpec((nseq, CONV_QKV_WIDTH - 1, W_QKV), grp3),
            pl.BlockSpec((nseq, N_HEADS, HEAD_K, HEAD_V), grp4),
        ],
        out_shape=[
            jax.ShapeDtypeStruct(x2.shape, F32),
            jax.ShapeDtypeStruct(buf_a.shape[1:], F32),
            jax.ShapeDtypeStruct(buf_qkv.shape[1:], F32),
            jax.ShapeDtypeStruct(s0.shape[1:], F32),
        ],
        scratch_shapes=[
            pltpu.VMEM((2 * tm, W_A), F32),
            pltpu.VMEM((2 * tm, W_QKV), F32),
            pltpu.VMEM((tm, W_VV), F32),
        ],
        compiler_params=pltpu.CompilerParams(
            dimension_semantics=("arbitrary",), vmem_limit_bytes=VMEM_LIMIT),
        name="mixer_step",
    )(x2, *weights, buf_a, buf_qkv, s0)
    return h2.reshape(x.shape), na, nq, ns


def _mlp(h, g2, w_up, w_down, gf, layer, final_norm):
    shape = h.shape
    h2 = h.reshape(-1, D_MODEL)
    rows = h2.shape[0]
    tm = min(MLP_TILE, rows)
    out = pl.pallas_call(
        functools.partial(_mlp_kernel, final_norm=final_norm),
        grid=(rows // tm,),
        in_specs=[
            pl.BlockSpec((tm, D_MODEL), lambda i: (i, 0)),
            _resident((1, D_MODEL), layer),
            _resident((D_MODEL, D_FF), layer),
            _resident((D_FF, D_MODEL), layer),
            _resident((1, D_MODEL), 0),
        ],
        out_specs=pl.BlockSpec((tm, D_MODEL), lambda i: (i, 0)),
        out_shape=jax.ShapeDtypeStruct(h2.shape, F32),
        compiler_params=pltpu.CompilerParams(
            dimension_semantics=("arbitrary",), vmem_limit_bytes=VMEM_LIMIT),
        name="mlp",
    )(h2, g2, w_up, w_down, gf)
    return out.reshape(shape)


def _head_rows(v):
    return jnp.pad(v.astype(F32), ((0, 0), (N_HEADS, LANES - 2 * N_HEADS)))[:, None, :]


def kernel(x_prompt, x_sample, state_conv_a, state_conv_qkv, state_delta, norm1_g, w_in, conv_a_w,
           w_a_out, conv_qkv_w, a_log, dt_bias, onorm_g, w_b_out, w_o, norm2_g, w_up, w_down,
           final_g):
    depth = w_in.shape[0]
    nb = x_prompt.shape[0]
    zero_a = jnp.zeros((nb, CONV_A_WIDTH - 1, W_A), F32)
    zero_qkv = jnp.zeros((nb, CONV_QKV_WIDTH - 1, W_QKV), F32)
    zero_s = jnp.zeros((nb, N_HEADS, HEAD_K, HEAD_V), F32)

    w_in_b = w_in.astype(BF16)
    w_small = jnp.concatenate(
        [w_in_b[:, :, OFF_G:OFF_G + LANES],
         jnp.pad(w_in_b[:, :, IN_WIDTH - GATE_SHIFT:], ((0, 0), (0, 0), (0, LANES - GATE_SHIFT)))], axis=2)
    weights = (
        norm1_g[:, None, :], w_in_b, w_small, conv_a_w,
        jnp.roll(w_a_out.astype(BF16), GATE_SHIFT, axis=2), conv_qkv_w,
        _head_rows(a_log), _head_rows(dt_bias), onorm_g[:, None, :],
        jnp.roll(w_b_out.astype(BF16), GATE_SHIFT, axis=2),
        jnp.roll(w_o.astype(BF16), GATE_SHIFT, axis=1))
    g2 = norm2_g[:, None, :]
    wu = w_up.astype(BF16)
    wd = w_down.astype(BF16)
    gf = final_g.reshape(1, 1, D_MODEL)

    xp, xs = x_prompt, x_sample
    outs_p = ([], [], [])
    outs_s = ([], [], [])
    for l in range(depth):
        last = l == depth - 1
        hp, pa, pq, ps = _mixer_stream(xp, zero_a, zero_qkv, zero_s, weights, l)
        xp = _mlp(hp, g2, wu, wd, gf, l, last)
        hs, sa, sq, ss = _mixer_step(xs, state_conv_a, state_conv_qkv, state_delta, weights, l)
        xs = _mlp(hs, g2, wu, wd, gf, l, last)
        for acc, val in zip(outs_p, (pa, pq, ps)):
            acc.append(val)
        for acc, val in zip(outs_s, (sa, sq, ss)):
            acc.append(val)

    return (xp, xs,
            jnp.stack(outs_p[0]), jnp.stack(outs_p[1]), jnp.stack(outs_p[2]),
            jnp.stack(outs_s[0]), jnp.stack(outs_s[1]), jnp.stack(outs_s[2]))
```

```python
import functools

import jax
import jax.numpy as jnp
from jax import lax
from jax.experimental import pallas as pl
from jax.experimental.pallas import tpu as pltpu

F32 = jnp.float32
BF16 = jnp.bfloat16

D_MODEL = 1024
W_A = 512
N_HEADS = 4
HEAD_K = 128
HEAD_V = 128
W_QK = N_HEADS * HEAD_K
W_VV = N_HEADS * HEAD_V
W_QKV = 2 * W_QK + W_VV
CONV_A_WIDTH = 3
CONV_QKV_WIDTH = 4
D_FF = 4 * D_MODEL
EPS = 1e-6

LANES = 128
SUBLANES = 8
BLOCK_ROWS = 128

OFF_A = 0
OFF_QKV = OFF_A + 3 * W_A
OFF_Z = OFF_QKV + W_QKV
OFF_G = OFF_Z + W_VV
GATE_SHIFT = 2 * N_HEADS
IN_WIDTH = OFF_G + GATE_SHIFT + 2 * D_MODEL

PROMPT_CHUNK = 64
MIXER_TILE = 512
STEP_TILE = 256
MLP_TILE = 1024
FF_CHUNK = 1024
PIECE = 256
VMEM_LIMIT = 56 * 1024 * 1024


def _mm(a, b):
    return jnp.dot(a.astype(BF16), b.astype(BF16), preferred_element_type=F32)


def _mm_nt(a, b):
    return lax.dot_general(a.astype(BF16), b.astype(BF16), (((1,), (1,)), ((), ())),
                           preferred_element_type=F32)


def _mm_tn(a, b):
    return lax.dot_general(a.astype(BF16), b.astype(BF16), (((0,), (0,)), ((), ())),
                           preferred_element_type=F32)


def _rms(x, g):
    return x * lax.rsqrt(jnp.mean(x * x, axis=-1, keepdims=True) + EPS) * g


def _sigmoid(x):
    return 1.0 / (1.0 + jnp.exp(-x))


def _softplus(x):
    return jnp.maximum(x, 0.0) + jnp.log1p(jnp.exp(-jnp.abs(x)))


def _chunk_scan(x, pos, chunk, reverse):
    rows = x.shape[0]
    s = 1
    while s < chunk:
        if reverse:
            x = x + jnp.where(pos < chunk - s, pltpu.roll(x, rows - s, 0), 0.0)
        else:
            x = x + jnp.where(pos >= s, pltpu.roll(x, s, 0), 0.0)
        s *= 2
    return x


def _mixer_kernel(*refs, tm, chunk, streams, n_alias):
    streaming = streams > 0
    (x_ref, g1_ref, win_ref, wsmall_ref, caw_ref, waout_ref, cqw_ref, alog_ref, dtb_ref, og_ref,
     wbout_ref, wo_ref, sa_ref, sq_ref, s0_ref) = refs[:15]
    refs = refs[15:]
    if streaming:
        xnext_ref = refs[0]
        h_ref, na_ref, nq_ref, ns_ref, pad_a, pad_q, o_scr, xn_scr = refs[1 + n_alias:]
    else:
        h_ref, na_ref, nq_ref, ns_ref, pad_a, pad_q, o_scr = refs[n_alias:]
    ts = tm // streams if streaming else tm
    nseq = tm // chunk
    row = lax.broadcasted_iota(jnp.int32, (tm, LANES), 0)
    pos = row & (chunk - 1)

    if streaming:
        @pl.when(pl.program_id(0) == 0)
        def _():
            ns_ref[...] = s0_ref[...]
            xn_scr[...] = _rms(x_ref[...].reshape(tm, D_MODEL), g1_ref[...]).astype(BF16)
            pad_a[...] = jnp.zeros(pad_a.shape, F32)
            pad_q[...] = jnp.zeros(pad_q.shape, F32)
            pad_a[:, SUBLANES - (CONV_A_WIDTH - 1):SUBLANES, :] = sa_ref[...]
            pad_q[:, SUBLANES - (CONV_QKV_WIDTH - 1):SUBLANES, :] = sq_ref[...]

    fillers = []

    def fill(n=1):
        for _ in range(min(n, len(fillers))):
            fillers.pop(0)()

    def queue_projection(lhs, w_ref, col0, width, out):
        def piece(j):
            def run():
                out[j] = _mm(lhs(), w_ref[:, col0 + j * PIECE:col0 + (j + 1) * PIECE])
            return run
        fillers.extend(piece(j) for j in range(width // PIECE))

    def causal_conv(pad, xin, w_ref, width, state_ref, new_ref, act):
        hw = width - 1
        taps = [w_ref[j:j + 1, :] for j in range(width)]
        if streaming:
            outs = []
            for b in range(streams):
                for r0 in range(b * ts, (b + 1) * ts, BLOCK_ROWS):
                    front = pad[b] if r0 == b * ts else xin[r0 - SUBLANES:r0]
                    slab = jnp.concatenate([front, xin[r0:r0 + BLOCK_ROWS]], axis=0)
                    y = xin[r0:r0 + BLOCK_ROWS] * taps[hw]
                    for s in range(1, width):
                        y = y + pltpu.roll(slab, s, 0)[SUBLANES:] * taps[hw - s]
                    outs.append(act(y))
                pad[b] = xin[(b + 1) * ts - SUBLANES:(b + 1) * ts]
                new_ref[b] = pad[b, SUBLANES - hw:SUBLANES, :]
            return jnp.concatenate(outs, axis=0)
        pad[0:tm, :] = xin
        pad[tm:2 * tm, :] = jnp.zeros((tm, xin.shape[1]), F32)
        for q in range(nseq):
            qp = (q - 1) % nseq
            pad[tm + qp * chunk + chunk - hw:tm + qp * chunk + chunk, :] = state_ref[q]
            new_ref[q] = pad[q * chunk + chunk - hw:q * chunk + chunk, :]
        hist = pad[tm:2 * tm, :]
        cpos = lax.broadcasted_iota(jnp.int32, (tm, 1), 0) & (chunk - 1)
        y = xin * taps[hw]
        for s in range(1, width):
            sh = jnp.where(cpos >= s, pltpu.roll(xin, s, 0), pltpu.roll(hist, s, 0))
            y = y + sh * taps[hw - s]
        return act(y)

    def wrap_lanes(piece, src):
        lane = lax.broadcasted_iota(jnp.int32, (tm, LANES), 1)
        return jnp.concatenate(
            [jnp.where(lane < GATE_SHIFT, src, piece[:, 0:LANES]), piece[:, LANES:]], axis=1)

    x = x_ref[...].reshape(tm, D_MODEL)
    xn = xn_scr[...] if streaming else _rms(x, g1_ref[...]).astype(BF16)

    pa_cols = [None] * (3 * W_A // PIECE)
    z_cols = [None] * (W_VV // PIECE)
    gate_cols = [None] * (2 * D_MODEL // PIECE)
    ya_cols = [None] * (D_MODEL // PIECE)
    queue_projection(lambda: xn, win_ref, OFF_A, 3 * W_A, pa_cols)
    queue_projection(lambda: xn, win_ref, OFF_Z, W_VV, z_cols)
    queue_projection(lambda: xn, win_ref, OFF_G, 2 * D_MODEL, gate_cols)

    p_q = _mm(xn, win_ref[:, OFF_QKV:OFF_QKV + W_QKV])
    small = _mm(xn, wsmall_ref[...])
    ba_raw = small[:, 0:LANES]
    gate_tail = small[:, LANES:]
    beta = _sigmoid(ba_raw)
    c = causal_conv(pad_q, p_q, cqw_ref, CONV_QKV_WIDTH, sq_ref, nq_ref, lambda y: y * _sigmoid(y))
    fill(len(pa_cols) + len(z_cols))

    g = -jnp.exp(alog_ref[...]) * _softplus(ba_raw + dtb_ref[...])
    gc = _chunk_scan(g, pos, chunk, reverse=False)
    g_after = _chunk_scan(g, pos, chunk, reverse=True) - g
    eg = jnp.exp(gc)
    e_after = jnp.exp(g_after)
    gc_t = gc.T

    ri = lax.broadcasted_iota(jnp.int32, (BLOCK_ROWS, BLOCK_ROWS), 0)
    ci = lax.broadcasted_iota(jnp.int32, (BLOCK_ROWS, BLOCK_ROWS), 1)
    m_incl = ((ri & -chunk) == (ci & -chunk)) & (ri >= ci)
    eye = jnp.where(ri == ci, 1.0, 0.0)
    m_levels = []
    size = 1
    while size < chunk:
        m_levels.append(((ri & -(2 * size)) == (ci & -(2 * size))) & ((ri & -size) > (ci & -size)))
        size *= 2

    pairs = [(rb, h) for rb in range(tm // BLOCK_ROWS) for h in range(N_HEADS)]
    q_g, k_t, rhs, p_blk, a_blk, t_inv = {}, {}, {}, {}, {}, {}
    for rb, h in pairs:
        rows = slice(rb * BLOCK_ROWS, (rb + 1) * BLOCK_ROWS)
        qh = c[rows, h * HEAD_K:(h + 1) * HEAD_K]
        kh = c[rows, W_QK + h * HEAD_K:W_QK + (h + 1) * HEAD_K]
        vh = c[rows, 2 * W_QK + h * HEAD_V:2 * W_QK + (h + 1) * HEAD_V]
        qh = qh * (lax.rsqrt(jnp.sum(qh * qh, axis=-1, keepdims=True) + EPS) * (HEAD_K ** -0.5))
        kh = kh * lax.rsqrt(jnp.sum(kh * kh, axis=-1, keepdims=True) + EPS)
        b_col = beta[rows, h:h + 1]
        hg = N_HEADS + h
        eg_col = eg[rows, hg:hg + 1]
        diff = gc[rows, hg:hg + 1] - gc_t[hg:hg + 1, rows]
        dec_incl = jnp.exp(jnp.where(m_incl, diff, -1e30))
        qk = _mm_nt(jnp.concatenate([qh, kh], axis=0), kh)
        p_blk[rb, h] = qk[0:BLOCK_ROWS] * dec_incl
        a_blk[rb, h] = qk[BLOCK_ROWS:] * dec_incl * b_col
        rhs[rb, h] = jnp.concatenate([vh * b_col, kh * (b_col * eg_col)], axis=1)
        q_g[rb, h] = qh * eg_col
        k_t[rb, h] = kh * e_after[rows, hg:hg + 1]
        t_inv[rb, h] = eye - jnp.where(m_levels[0], a_blk[rb, h], 0.0)

    n_gp = D_MODEL // PIECE
    fill(n_gp + 1)

    p_a = jnp.concatenate(pa_cols, axis=1)
    gated_u = causal_conv(pad_a, p_a[:, W_A:2 * W_A] * p_a[:, 2 * W_A:3 * W_A], caw_ref,
                          CONV_A_WIDTH, sa_ref, na_ref, lambda y: y) * p_a[:, 0:W_A]
    gated_u = gated_u.astype(BF16)
    queue_projection(lambda: gated_u, waout_ref, 0, D_MODEL, ya_cols)
    z = jnp.concatenate(z_cols, axis=1)
    z_act = z * _sigmoid(z)
    sig_a = _sigmoid(jnp.concatenate(
        [wrap_lanes(gate_cols[0], gate_cols[n_gp][:, 0:LANES])] + gate_cols[1:n_gp], axis=1))

    for m_off in m_levels[1:]:
        a_t = {p: _mm(jnp.where(m_off, a_blk[p], 0.0), t_inv[p]) for p in pairs}
        t_inv = {p: t_inv[p] - _mm(t_inv[p], a_t[p]) for p in pairs}
    w_vk = {p: _mm(t_inv[p], rhs[p]) for p in pairs}

    cpb = BLOCK_ROWS // chunk
    if streaming:
        cps = ts // chunk
        groups = [[(b * cps + j, h, b) for b in range(streams) for h in range(N_HEADS)]
                  for j in range(cps)]
    else:
        groups = [[(q, h, q) for q in range(nseq) for h in range(N_HEADS)]]
    for group in groups:
        s_old, xs, u_c = {}, {}, {}
        for q, h, slot in group:
            p, c0 = (q // cpb, h), (q % cpb) * chunk
            s_old[q, h] = ns_ref[slot, h] if streaming else s0_ref[slot, h]
            xs[q, h] = _mm(jnp.concatenate([w_vk[p][c0:c0 + chunk, HEAD_V:], q_g[p][c0:c0 + chunk]],
                                           axis=0), s_old[q, h])
        fill()
        for q, h, slot in group:
            p, c0 = (q // cpb, h), (q % cpb) * chunk
            u_c[q, h] = w_vk[p][c0:c0 + chunk, 0:HEAD_V] - xs[q, h][0:chunk]
            o_c = xs[q, h][chunk:] + _mm(p_blk[p][c0:c0 + chunk, c0:c0 + chunk], u_c[q, h])
            o_scr[q * chunk:(q + 1) * chunk, h * HEAD_V:(h + 1) * HEAD_V] = o_c
        for q, h, slot in group:
            p, c0 = (q // cpb, h), (q % cpb) * chunk
            g_last = eg[(q + 1) * chunk - 1:(q + 1) * chunk, N_HEADS + h:N_HEADS + h + 1]
            ns_ref[slot, h] = s_old[q, h] * g_last + _mm_tn(k_t[p][c0:c0 + chunk], u_c[q, h])
        fill()
    fill(len(fillers))

    og = og_ref[...]
    o_parts = []
    for h in range(N_HEADS):
        o_h = o_scr[:, h * HEAD_V:(h + 1) * HEAD_V]
        o_parts.append(_rms(o_h, og) * z_act[:, h * HEAD_V:(h + 1) * HEAD_V])
    y_b = _mm(jnp.concatenate(o_parts, axis=1), wbout_ref[...])

    sig_b = _sigmoid(jnp.concatenate(
        [wrap_lanes(gate_cols[n_gp], gate_tail)] + gate_cols[n_gp + 1:], axis=1))
    mixed = sig_a * jnp.concatenate(ya_cols, axis=1) + sig_b * y_b
    if streaming:
        xn_scr[...] = _rms(xnext_ref[...].reshape(tm, D_MODEL), g1_ref[...]).astype(BF16)
    h_ref[...] = (x + _mm(mixed, wo_ref[...])).reshape(h_ref.shape)


def _mlp_kernel(h_ref, g2_ref, wup_ref, wdown_ref, gf_ref, out_ref, *, final_norm):
    h = h_ref[...]
    hn = _rms(h, g2_ref[...]).astype(BF16)
    acc = h
    for j in range(D_FF // FF_CHUNK):
        up = jnp.dot(hn, wup_ref[:, j * FF_CHUNK:(j + 1) * FF_CHUNK], preferred_element_type=F32)
        act = jnp.maximum(up, 0.0)
        acc = acc + _mm(act * act, wdown_ref[j * FF_CHUNK:(j + 1) * FF_CHUNK, :])
    if final_norm:
        acc = _rms(acc, gf_ref[...])
    out_ref[...] = acc


def _resident(shape, layer):
    nd = len(shape)
    return pl.BlockSpec((None,) + shape, lambda *_: (layer,) + (0,) * nd,
                        pipeline_mode=pl.Buffered(1))


def _mixer_weight_specs(layer):
    return [
        _resident((1, D_MODEL), layer),
        _resident((D_MODEL, IN_WIDTH), layer),
        _resident((D_MODEL, 2 * LANES), layer),
        _resident((CONV_A_WIDTH, W_A), layer),
        _resident((W_A, D_MODEL), layer),
        _resident((CONV_QKV_WIDTH, W_QKV), layer),
        _resident((1, LANES), layer),
        _resident((1, LANES), layer),
        _resident((1, HEAD_V), layer),
        _resident((W_VV, D_MODEL), layer),
        _resident((D_MODEL, D_MODEL), layer),
    ]


def _alias_kwargs(n_inputs, stacked_prev):
    if stacked_prev is None:
        return [], [], {}
    specs = [pl.BlockSpec(memory_space=pl.ANY)] * len(stacked_prev)
    aliases = {n_inputs + k: 1 + k for k in range(len(stacked_prev))}
    return list(stacked_prev), specs, aliases


def _mixer_stream(x, buf_a, buf_qkv, s0, weights, layer, depth, stacked_prev):
    nb, seq, _ = x.shape
    tm = MIXER_TILE
    ts = tm // nb
    n_steps = seq // ts
    whole3 = lambda i: (0, 0, 0)
    whole4 = lambda i: (0, 0, 0, 0)
    state_specs = [
        pl.BlockSpec((nb, CONV_A_WIDTH - 1, W_A), whole3),
        pl.BlockSpec((nb, CONV_QKV_WIDTH - 1, W_QKV), whole3),
        pl.BlockSpec((nb, N_HEADS, HEAD_K, HEAD_V), whole4),
    ]
    new_state_specs = [
        pl.BlockSpec((None, nb, CONV_A_WIDTH - 1, W_A), lambda i: (layer, 0, 0, 0)),
        pl.BlockSpec((None, nb, CONV_QKV_WIDTH - 1, W_QKV), lambda i: (layer, 0, 0, 0)),
        pl.BlockSpec((None, nb, N_HEADS, HEAD_K, HEAD_V), lambda i: (layer, 0, 0, 0, 0)),
    ]
    in_specs = ([pl.BlockSpec((nb, ts, D_MODEL), lambda i: (0, i, 0))]
                + _mixer_weight_specs(layer) + state_specs
                + [pl.BlockSpec((nb, ts, D_MODEL), lambda i: (0, jnp.minimum(i + 1, n_steps - 1), 0))])
    prev, prev_specs, aliases = _alias_kwargs(len(in_specs), stacked_prev)
    kern = functools.partial(_mixer_kernel, tm=tm, chunk=PROMPT_CHUNK, streams=nb, n_alias=len(prev))
    return pl.pallas_call(
        kern,
        grid=(n_steps,),
        in_specs=in_specs + prev_specs,
        out_specs=[pl.BlockSpec((nb, ts, D_MODEL), lambda i: (0, i, 0))] + new_state_specs,
        out_shape=[
            jax.ShapeDtypeStruct(x.shape, F32),
            jax.ShapeDtypeStruct((depth,) + buf_a.shape, F32),
            jax.ShapeDtypeStruct((depth,) + buf_qkv.shape, F32),
            jax.ShapeDtypeStruct((depth,) + s0.shape, F32),
        ],
        input_output_aliases=aliases,
        scratch_shapes=[
            pltpu.VMEM((nb, SUBLANES, W_A), F32),
            pltpu.VMEM((nb, SUBLANES, W_QKV), F32),
            pltpu.VMEM((tm, W_VV), F32),
            pltpu.VMEM((tm, D_MODEL), BF16),
        ],
        compiler_params=pltpu.CompilerParams(
            dimension_semantics=("arbitrary",), vmem_limit_bytes=VMEM_LIMIT),
        name="mixer_stream",
    )(x, *weights, buf_a, buf_qkv, s0, x, *prev)


def _mixer_step(x, buf_a, buf_qkv, s0, weights, layer, stacked_prev):
    nb, seq, _ = x.shape
    tm = STEP_TILE
    nseq = tm // seq
    lay3 = lambda i: (layer, i, 0, 0)
    lay4 = lambda i: (layer, i, 0, 0, 0)
    x2 = x.reshape(nb * seq, D_MODEL)
    state_specs = [
        pl.BlockSpec((None, nseq, CONV_A_WIDTH - 1, W_A), lay3),
        pl.BlockSpec((None, nseq, CONV_QKV_WIDTH - 1, W_QKV), lay3),
        pl.BlockSpec((None, nseq, N_HEADS, HEAD_K, HEAD_V), lay4),
    ]
    in_specs = [pl.BlockSpec((tm, D_MODEL), lambda i: (i, 0))] + _mixer_weight_specs(layer) + state_specs
    prev, prev_specs, aliases = _alias_kwargs(len(in_specs), stacked_prev)
    kern = functools.partial(_mixer_kernel, tm=tm, chunk=seq, streams=0, n_alias=len(prev))
    h2, na, nq, ns = pl.pallas_call(
        kern,
        grid=(nb // nseq,),
        in_specs=in_specs + prev_specs,
        out_specs=[pl.BlockSpec((tm, D_MODEL), lambda i: (i, 0))] + state_specs,
        out_shape=[
            jax.ShapeDtypeStruct(x2.shape, F32),
            jax.ShapeDtypeStruct(buf_a.shape, F32),
            jax.ShapeDtypeStruct(buf_qkv.shape, F32),
            jax.ShapeDtypeStruct(s0.shape, F32),
        ],
        input_output_aliases=aliases,
        scratch_shapes=[
            pltpu.VMEM((2 * tm, W_A), F32),
            pltpu.VMEM((2 * tm, W_QKV), F32),
            pltpu.VMEM((tm, W_VV), F32),
        ],
        compiler_params=pltpu.CompilerParams(
            dimension_semantics=("arbitrary",), vmem_limit_bytes=VMEM_LIMIT),
        name="mixer_step",
    )(x2, *weights, buf_a, buf_qkv, s0, *prev)
    return h2.reshape(x.shape), na, nq, ns


def _mlp(h, g2, w_up, w_down, gf, layer, final_norm):
    shape = h.shape
    h2 = h.reshape(-1, D_MODEL)
    rows = h2.shape[0]
    tm = min(MLP_TILE, rows)
    out = pl.pallas_call(
        functools.partial(_mlp_kernel, final_norm=final_norm),
        grid=(rows // tm,),
        in_specs=[
            pl.BlockSpec((tm, D_MODEL), lambda i: (i, 0)),
            _resident((1, D_MODEL), layer),
            _resident((D_MODEL, D_FF), layer),
            _resident((D_FF, D_MODEL), layer),
            _resident((1, D_MODEL), 0),
        ],
        out_specs=pl.BlockSpec((tm, D_MODEL), lambda i: (i, 0)),
        out_shape=jax.ShapeDtypeStruct(h2.shape, F32),
        compiler_params=pltpu.CompilerParams(
            dimension_semantics=("arbitrary",), vmem_limit_bytes=VMEM_LIMIT),
        name="mlp",
    )(h2, g2, w_up, w_down, gf)
    return out.reshape(shape)


def _head_rows(v):
    return jnp.pad(v.astype(F32), ((0, 0), (N_HEADS, LANES - 2 * N_HEADS)))[:, None, :]


def kernel(x_prompt, x_sample, state_conv_a, state_conv_qkv, state_delta, norm1_g, w_in, conv_a_w,
           w_a_out, conv_qkv_w, a_log, dt_bias, onorm_g, w_b_out, w_o, norm2_g, w_up, w_down,
           final_g):
    depth = w_in.shape[0]
    nb = x_prompt.shape[0]
    zero_a = jnp.zeros((nb, CONV_A_WIDTH - 1, W_A), F32)
    zero_qkv = jnp.zeros((nb, CONV_QKV_WIDTH - 1, W_QKV), F32)
    zero_s = jnp.zeros((nb, N_HEADS, HEAD_K, HEAD_V), F32)

    w_in_b = w_in.astype(BF16)
    w_small = jnp.concatenate(
        [w_in_b[:, :, OFF_G:OFF_G + LANES],
         jnp.pad(w_in_b[:, :, IN_WIDTH - GATE_SHIFT:], ((0, 0), (0, 0), (0, LANES - GATE_SHIFT)))], axis=2)
    weights = (
        norm1_g[:, None, :], w_in_b, w_small, conv_a_w,
        jnp.roll(w_a_out.astype(BF16), GATE_SHIFT, axis=2), conv_qkv_w,
        _head_rows(a_log), _head_rows(dt_bias), onorm_g[:, None, :],
        jnp.roll(w_b_out.astype(BF16), GATE_SHIFT, axis=2),
        jnp.roll(w_o.astype(BF16), GATE_SHIFT, axis=1))
    g2 = norm2_g[:, None, :]
    wu = w_up.astype(BF16)
    wd = w_down.astype(BF16)
    gf = final_g.reshape(1, 1, D_MODEL)

    xp, xs = x_prompt, x_sample
    new_p = new_s = None
    for l in range(depth):
        last = l == depth - 1
        hp, *new_p = _mixer_stream(xp, zero_a, zero_qkv, zero_s, weights, l, depth, new_p)
        xp = _mlp(hp, g2, wu, wd, gf, l, last)
        hs, *new_s = _mixer_step(xs, state_conv_a, state_conv_qkv, state_delta, weights, l, new_s)
        xs = _mlp(hs, g2, wu, wd, gf, l, last)

    return (xp, xs, *new_p, *new_s)
```

```python
import functools

import jax
import jax.numpy as jnp
from jax import lax
from jax.experimental import pallas as pl
from jax.experimental.pallas import tpu as pltpu

F32 = jnp.float32
BF16 = jnp.bfloat16

D_MODEL = 1024
W_A = 512
N_HEADS = 4
HEAD_K = 128
HEAD_V = 128
W_QK = N_HEADS * HEAD_K
W_VV = N_HEADS * HEAD_V
W_QKV = 2 * W_QK + W_VV
CONV_A_WIDTH = 3
CONV_QKV_WIDTH = 4
D_FF = 4 * D_MODEL
EPS = 1e-6

LANES = 128
SUBLANES = 8
BLOCK_ROWS = 128

OFF_A = 0
OFF_QKV = OFF_A + 3 * W_A
OFF_Z = OFF_QKV + W_QKV
OFF_G = OFF_Z + W_VV
GATE_SHIFT = 2 * N_HEADS
IN_WIDTH = OFF_G + GATE_SHIFT + 2 * D_MODEL

PROMPT_CHUNK = 64
MIXER_TILE = 512
STEP_TILE = 256
MLP_TILE = 1024
FF_CHUNK = 1024
PIECE = 256
VMEM_LIMIT = 56 * 1024 * 1024


def _mm(a, b):
    return jnp.dot(a.astype(BF16), b.astype(BF16), preferred_element_type=F32)


def _mm_nt(a, b):
    return lax.dot_general(a.astype(BF16), b.astype(BF16), (((1,), (1,)), ((), ())),
                           preferred_element_type=F32)


def _mm_tn(a, b):
    return lax.dot_general(a.astype(BF16), b.astype(BF16), (((0,), (0,)), ((), ())),
                           preferred_element_type=F32)


def _rms(x, g):
    return x * lax.rsqrt(jnp.mean(x * x, axis=-1, keepdims=True) + EPS) * g


def _sigmoid(x):
    return 1.0 / (1.0 + jnp.exp(-x))


def _softplus(x):
    return jnp.maximum(x, 0.0) + jnp.log1p(jnp.exp(-jnp.abs(x)))


def _chunk_scan(x, pos, chunk, reverse):
    rows = x.shape[0]
    s = 1
    while s < chunk:
        if reverse:
            x = x + jnp.where(pos < chunk - s, pltpu.roll(x, rows - s, 0), 0.0)
        else:
            x = x + jnp.where(pos >= s, pltpu.roll(x, s, 0), 0.0)
        s *= 2
    return x


def _mixer_kernel(*refs, tm, chunk, streams, n_alias):
    streaming = streams > 0
    (x_ref, g1_ref, win_ref, wsmall_ref, caw_ref, waout_ref, cqw_ref, alog_ref, dtb_ref, og_ref,
     wbout_ref, wo_ref, sa_ref, sq_ref, s0_ref) = refs[:15]
    refs = refs[15:]
    if streaming:
        xnext_ref, wup_f32_ref, wdown_f32_ref = refs[:3]
        (h_ref, na_ref, nq_ref, ns_ref, wup_bf_ref, wdown_bf_ref,
         pad_a, pad_q, o_scr, xn_scr) = refs[3 + n_alias:]
    else:
        h_ref, na_ref, nq_ref, ns_ref, pad_a, pad_q, o_scr = refs[n_alias:]
    ts = tm // streams if streaming else tm
    nseq = tm // chunk
    row = lax.broadcasted_iota(jnp.int32, (tm, LANES), 0)
    pos = row & (chunk - 1)

    if streaming:
        @pl.when(pl.program_id(0) == 0)
        def _():
            ns_ref[...] = s0_ref[...]
            xn_scr[...] = _rms(x_ref[...].reshape(tm, D_MODEL), g1_ref[...]).astype(BF16)
            pad_a[...] = jnp.zeros(pad_a.shape, F32)
            pad_q[...] = jnp.zeros(pad_q.shape, F32)
            pad_a[:, SUBLANES - (CONV_A_WIDTH - 1):SUBLANES, :] = sa_ref[...]
            pad_q[:, SUBLANES - (CONV_QKV_WIDTH - 1):SUBLANES, :] = sq_ref[...]

    fillers = []

    def fill(n=1):
        for _ in range(min(n, len(fillers))):
            fillers.pop(0)()

    def queue_projection(lhs, w_ref, col0, width, out):
        def piece(j):
            def run():
                out[j] = _mm(lhs(), w_ref[:, col0 + j * PIECE:col0 + (j + 1) * PIECE])
            return run
        fillers.extend(piece(j) for j in range(width // PIECE))

    def causal_conv(pad, xin, w_ref, width, state_ref, new_ref, act):
        hw = width - 1
        taps = [w_ref[j:j + 1, :] for j in range(width)]
        if streaming:
            outs = []
            for b in range(streams):
                for r0 in range(b * ts, (b + 1) * ts, BLOCK_ROWS):
                    front = pad[b] if r0 == b * ts else xin[r0 - SUBLANES:r0]
                    slab = jnp.concatenate([front, xin[r0:r0 + BLOCK_ROWS]], axis=0)
                    y = xin[r0:r0 + BLOCK_ROWS] * taps[hw]
                    for s in range(1, width):
                        y = y + pltpu.roll(slab, s, 0)[SUBLANES:] * taps[hw - s]
                    outs.append(act(y))
                pad[b] = xin[(b + 1) * ts - SUBLANES:(b + 1) * ts]
                new_ref[b] = pad[b, SUBLANES - hw:SUBLANES, :]
            return jnp.concatenate(outs, axis=0)
        pad[0:tm, :] = xin
        pad[tm:2 * tm, :] = jnp.zeros((tm, xin.shape[1]), F32)
        for q in range(nseq):
            qp = (q - 1) % nseq
            pad[tm + qp * chunk + chunk - hw:tm + qp * chunk + chunk, :] = state_ref[q]
            new_ref[q] = pad[q * chunk + chunk - hw:q * chunk + chunk, :]
        hist = pad[tm:2 * tm, :]
        cpos = lax.broadcasted_iota(jnp.int32, (tm, 1), 0) & (chunk - 1)
        y = xin * taps[hw]
        for s in range(1, width):
            sh = jnp.where(cpos >= s, pltpu.roll(xin, s, 0), pltpu.roll(hist, s, 0))
            y = y + sh * taps[hw - s]
        return act(y)

    def wrap_lanes(piece, src):
        lane = lax.broadcasted_iota(jnp.int32, (tm, LANES), 1)
        return jnp.concatenate(
            [jnp.where(lane < GATE_SHIFT, src, piece[:, 0:LANES]), piece[:, LANES:]], axis=1)

    x = x_ref[...].reshape(tm, D_MODEL)
    xn = xn_scr[...] if streaming else _rms(x, g1_ref[...]).astype(BF16)

    pa_cols = [None] * (3 * W_A // PIECE)
    z_cols = [None] * (W_VV // PIECE)
    gate_cols = [None] * (2 * D_MODEL // PIECE)
    ya_cols = [None] * (D_MODEL // PIECE)
    queue_projection(lambda: xn, win_ref, OFF_A, 3 * W_A, pa_cols)
    queue_projection(lambda: xn, win_ref, OFF_Z, W_VV, z_cols)
    queue_projection(lambda: xn, win_ref, OFF_G, 2 * D_MODEL, gate_cols)

    p_q = _mm(xn, win_ref[:, OFF_QKV:OFF_QKV + W_QKV])
    small = _mm(xn, wsmall_ref[...])
    ba_raw = small[:, 0:LANES]
    gate_tail = small[:, LANES:]
    beta = _sigmoid(ba_raw)
    c = causal_conv(pad_q, p_q, cqw_ref, CONV_QKV_WIDTH, sq_ref, nq_ref, lambda y: y * _sigmoid(y))
    fill(len(pa_cols) + len(z_cols))

    g = -jnp.exp(alog_ref[...]) * _softplus(ba_raw + dtb_ref[...])
    gc = _chunk_scan(g, pos, chunk, reverse=False)
    g_after = _chunk_scan(g, pos, chunk, reverse=True) - g
    eg = jnp.exp(gc)
    e_after = jnp.exp(g_after)
    gc_t = gc.T

    ri = lax.broadcasted_iota(jnp.int32, (BLOCK_ROWS, BLOCK_ROWS), 0)
    ci = lax.broadcasted_iota(jnp.int32, (BLOCK_ROWS, BLOCK_ROWS), 1)
    m_incl = ((ri & -chunk) == (ci & -chunk)) & (ri >= ci)
    eye = jnp.where(ri == ci, 1.0, 0.0)
    m_levels = []
    size = 1
    while size < chunk:
        m_levels.append(((ri & -(2 * size)) == (ci & -(2 * size))) & ((ri & -size) > (ci & -size)))
        size *= 2

    pairs = [(rb, h) for rb in range(tm // BLOCK_ROWS) for h in range(N_HEADS)]
    q_g, k_t, rhs, p_blk, a_blk, t_inv = {}, {}, {}, {}, {}, {}
    for rb, h in pairs:
        rows = slice(rb * BLOCK_ROWS, (rb + 1) * BLOCK_ROWS)
        qh = c[rows, h * HEAD_K:(h + 1) * HEAD_K]
        kh = c[rows, W_QK + h * HEAD_K:W_QK + (h + 1) * HEAD_K]
        vh = c[rows, 2 * W_QK + h * HEAD_V:2 * W_QK + (h + 1) * HEAD_V]
        qh = qh * (lax.rsqrt(jnp.sum(qh * qh, axis=-1, keepdims=True) + EPS) * (HEAD_K ** -0.5))
        kh = kh * lax.rsqrt(jnp.sum(kh * kh, axis=-1, keepdims=True) + EPS)
        b_col = beta[rows, h:h + 1]
        hg = N_HEADS + h
        eg_col = eg[rows, hg:hg + 1]
        diff = gc[rows, hg:hg + 1] - gc_t[hg:hg + 1, rows]
        dec_incl = jnp.exp(jnp.where(m_incl, diff, -1e30))
        qk = _mm_nt(jnp.concatenate([qh, kh], axis=0), kh)
        p_blk[rb, h] = qk[0:BLOCK_ROWS] * dec_incl
        a_blk[rb, h] = qk[BLOCK_ROWS:] * dec_incl * b_col
        rhs[rb, h] = jnp.concatenate([vh * b_col, kh * (b_col * eg_col)], axis=1)
        q_g[rb, h] = qh * eg_col
        k_t[rb, h] = kh * e_after[rows, hg:hg + 1]
        t_inv[rb, h] = eye - jnp.where(m_levels[0], a_blk[rb, h], 0.0)

    n_gp = D_MODEL // PIECE
    fill(n_gp + 1)

    p_a = jnp.concatenate(pa_cols, axis=1)
    gated_u = causal_conv(pad_a, p_a[:, W_A:2 * W_A] * p_a[:, 2 * W_A:3 * W_A], caw_ref,
                          CONV_A_WIDTH, sa_ref, na_ref, lambda y: y) * p_a[:, 0:W_A]
    gated_u = gated_u.astype(BF16)
    queue_projection(lambda: gated_u, waout_ref, 0, D_MODEL, ya_cols)
    z = jnp.concatenate(z_cols, axis=1)
    z_act = z * _sigmoid(z)
    sig_a = _sigmoid(jnp.concatenate(
        [wrap_lanes(gate_cols[0], gate_cols[n_gp][:, 0:LANES])] + gate_cols[1:n_gp], axis=1))

    for m_off in m_levels[1:]:
        a_t = {p: _mm(jnp.where(m_off, a_blk[p], 0.0), t_inv[p]) for p in pairs}
        t_inv = {p: t_inv[p] - _mm(t_inv[p], a_t[p]) for p in pairs}
    w_vk = {p: _mm(t_inv[p], rhs[p]) for p in pairs}

    cpb = BLOCK_ROWS // chunk
    if streaming:
        cps = ts // chunk
        groups = [[(b * cps + j, h, b) for b in range(streams) for h in range(N_HEADS)]
                  for j in range(cps)]
    else:
        groups = [[(q, h, q) for q in range(nseq) for h in range(N_HEADS)]]
    for group in groups:
        s_old, xs, u_c = {}, {}, {}
        for q, h, slot in group:
            p, c0 = (q // cpb, h), (q % cpb) * chunk
            s_old[q, h] = ns_ref[slot, h] if streaming else s0_ref[slot, h]
            xs[q, h] = _mm(jnp.concatenate([w_vk[p][c0:c0 + chunk, HEAD_V:], q_g[p][c0:c0 + chunk]],
                                           axis=0), s_old[q, h])
        fill()
        for q, h, slot in group:
            p, c0 = (q // cpb, h), (q % cpb) * chunk
            u_c[q, h] = w_vk[p][c0:c0 + chunk, 0:HEAD_V] - xs[q, h][0:chunk]
            o_c = xs[q, h][chunk:] + _mm(p_blk[p][c0:c0 + chunk, c0:c0 + chunk], u_c[q, h])
            o_scr[q * chunk:(q + 1) * chunk, h * HEAD_V:(h + 1) * HEAD_V] = o_c
        for q, h, slot in group:
            p, c0 = (q // cpb, h), (q % cpb) * chunk
            g_last = eg[(q + 1) * chunk - 1:(q + 1) * chunk, N_HEADS + h:N_HEADS + h + 1]
            ns_ref[slot, h] = s_old[q, h] * g_last + _mm_tn(k_t[p][c0:c0 + chunk], u_c[q, h])
        fill()
    fill(len(fillers))

    og = og_ref[...]
    o_parts = []
    for h in range(N_HEADS):
        o_h = o_scr[:, h * HEAD_V:(h + 1) * HEAD_V]
        o_parts.append(_rms(o_h, og) * z_act[:, h * HEAD_V:(h + 1) * HEAD_V])
    y_b = _mm(jnp.concatenate(o_parts, axis=1), wbout_ref[...])

    sig_b = _sigmoid(jnp.concatenate(
        [wrap_lanes(gate_cols[n_gp], gate_tail)] + gate_cols[n_gp + 1:], axis=1))
    mixed = sig_a * jnp.concatenate(ya_cols, axis=1) + sig_b * y_b
    if streaming:
        xn_scr[...] = _rms(xnext_ref[...].reshape(tm, D_MODEL), g1_ref[...]).astype(BF16)
        wup_bf_ref[...] = wup_f32_ref[...].astype(BF16)
        wdown_bf_ref[...] = wdown_f32_ref[...].astype(BF16)
    h_ref[...] = (x + _mm(mixed, wo_ref[...])).reshape(h_ref.shape)


def _mlp_kernel(h_ref, g2_ref, wup_ref, wdown_ref, gf_ref, *rest, final_norm):
    out_ref = rest[-1] if len(rest) == 1 else rest[1]
    if len(rest) == 3:
        rest[2][...] = rest[0][...].astype(BF16)
    h = h_ref[...]
    hn = _rms(h, g2_ref[...]).astype(BF16)
    acc = h
    for j in range(D_FF // FF_CHUNK):
        up = jnp.dot(hn, wup_ref[:, j * FF_CHUNK:(j + 1) * FF_CHUNK], preferred_element_type=F32)
        act = jnp.maximum(up, 0.0)
        acc = acc + _mm(act * act, wdown_ref[j * FF_CHUNK:(j + 1) * FF_CHUNK, :])
    if final_norm:
        acc = _rms(acc, gf_ref[...])
    out_ref[...] = acc


def _resident(shape, layer):
    nd = len(shape)
    return pl.BlockSpec((None,) + shape, lambda *_: (layer,) + (0,) * nd,
                        pipeline_mode=pl.Buffered(1))


def _mixer_weight_specs(layer):
    return [
        _resident((1, D_MODEL), layer),
        _resident((D_MODEL, IN_WIDTH), 0),
        _resident((D_MODEL, 2 * LANES), layer),
        _resident((CONV_A_WIDTH, W_A), layer),
        _resident((W_A, D_MODEL), layer),
        _resident((CONV_QKV_WIDTH, W_QKV), layer),
        _resident((1, LANES), layer),
        _resident((1, LANES), layer),
        _resident((1, HEAD_V), layer),
        _resident((W_VV, D_MODEL), layer),
        _resident((D_MODEL, D_MODEL), layer),
    ]


def _alias_kwargs(n_inputs, stacked_prev):
    if stacked_prev is None:
        return [], [], {}
    specs = [pl.BlockSpec(memory_space=pl.ANY)] * len(stacked_prev)
    aliases = {n_inputs + k: 1 + k for k in range(len(stacked_prev))}
    return list(stacked_prev), specs, aliases


def _mixer_stream(x, buf_a, buf_qkv, s0, weights, layer, depth, stacked_prev, w_up, w_down):
    nb, seq, _ = x.shape
    tm = MIXER_TILE
    ts = tm // nb
    n_steps = seq // ts
    ff_blk = D_FF // n_steps
    whole3 = lambda i: (0, 0, 0)
    whole4 = lambda i: (0, 0, 0, 0)
    state_specs = [
        pl.BlockSpec((nb, CONV_A_WIDTH - 1, W_A), whole3),
        pl.BlockSpec((nb, CONV_QKV_WIDTH - 1, W_QKV), whole3),
        pl.BlockSpec((nb, N_HEADS, HEAD_K, HEAD_V), whole4),
    ]
    new_state_specs = [
        pl.BlockSpec((None, nb, CONV_A_WIDTH - 1, W_A), lambda i: (layer, 0, 0, 0)),
        pl.BlockSpec((None, nb, CONV_QKV_WIDTH - 1, W_QKV), lambda i: (layer, 0, 0, 0)),
        pl.BlockSpec((None, nb, N_HEADS, HEAD_K, HEAD_V), lambda i: (layer, 0, 0, 0, 0)),
    ]
    in_specs = ([pl.BlockSpec((nb, ts, D_MODEL), lambda i: (0, i, 0))]
                + _mixer_weight_specs(layer) + state_specs
                + [pl.BlockSpec((nb, ts, D_MODEL), lambda i: (0, jnp.minimum(i + 1, n_steps - 1), 0)),
                   pl.BlockSpec((None, D_MODEL, ff_blk), lambda i: (layer, 0, i)),
                   pl.BlockSpec((None, ff_blk, D_MODEL), lambda i: (layer, i, 0))])
    prev, prev_specs, aliases = _alias_kwargs(len(in_specs), stacked_prev)
    kern = functools.partial(_mixer_kernel, tm=tm, chunk=PROMPT_CHUNK, streams=nb, n_alias=len(prev))
    return pl.pallas_call(
        kern,
        grid=(n_steps,),
        in_specs=in_specs + prev_specs,
        out_specs=[pl.BlockSpec((nb, ts, D_MODEL), lambda i: (0, i, 0))] + new_state_specs + [
            pl.BlockSpec((None, D_MODEL, ff_blk), lambda i: (0, 0, i)),
            pl.BlockSpec((None, ff_blk, D_MODEL), lambda i: (0, i, 0)),
        ],
        out_shape=[
            jax.ShapeDtypeStruct(x.shape, F32),
            jax.ShapeDtypeStruct((depth,) + buf_a.shape, F32),
            jax.ShapeDtypeStruct((depth,) + buf_qkv.shape, F32),
            jax.ShapeDtypeStruct((depth,) + s0.shape, F32),
            jax.ShapeDtypeStruct((1, D_MODEL, D_FF), BF16),
            jax.ShapeDtypeStruct((1, D_FF, D_MODEL), BF16),
        ],
        input_output_aliases=aliases,
        scratch_shapes=[
            pltpu.VMEM((nb, SUBLANES, W_A), F32),
            pltpu.VMEM((nb, SUBLANES, W_QKV), F32),
            pltpu.VMEM((tm, W_VV), F32),
            pltpu.VMEM((tm, D_MODEL), BF16),
        ],
        compiler_params=pltpu.CompilerParams(
            dimension_semantics=("arbitrary",), vmem_limit_bytes=VMEM_LIMIT),
        name="mixer_stream",
    )(x, *weights, buf_a, buf_qkv, s0, x, w_up, w_down, *prev)


def _mixer_step(x, buf_a, buf_qkv, s0, weights, layer, stacked_prev):
    nb, seq, _ = x.shape
    tm = STEP_TILE
    nseq = tm // seq
    lay3 = lambda i: (layer, i, 0, 0)
    lay4 = lambda i: (layer, i, 0, 0, 0)
    x2 = x.reshape(nb * seq, D_MODEL)
    state_specs = [
        pl.BlockSpec((None, nseq, CONV_A_WIDTH - 1, W_A), lay3),
        pl.BlockSpec((None, nseq, CONV_QKV_WIDTH - 1, W_QKV), lay3),
        pl.BlockSpec((None, nseq, N_HEADS, HEAD_K, HEAD_V), lay4),
    ]
    in_specs = [pl.BlockSpec((tm, D_MODEL), lambda i: (i, 0))] + _mixer_weight_specs(layer) + state_specs
    prev, prev_specs, aliases = _alias_kwargs(len(in_specs), stacked_prev)
    kern = functools.partial(_mixer_kernel, tm=tm, chunk=seq, streams=0, n_alias=len(prev))
    h2, na, nq, ns = pl.pallas_call(
        kern,
        grid=(nb // nseq,),
        in_specs=in_specs + prev_specs,
        out_specs=[pl.BlockSpec((tm, D_MODEL), lambda i: (i, 0))] + state_specs,
        out_shape=[
            jax.ShapeDtypeStruct(x2.shape, F32),
            jax.ShapeDtypeStruct(buf_a.shape, F32),
            jax.ShapeDtypeStruct(buf_qkv.shape, F32),
            jax.ShapeDtypeStruct(s0.shape, F32),
        ],
        input_output_aliases=aliases,
        scratch_shapes=[
            pltpu.VMEM((2 * tm, W_A), F32),
            pltpu.VMEM((2 * tm, W_QKV), F32),
            pltpu.VMEM((tm, W_VV), F32),
        ],
        compiler_params=pltpu.CompilerParams(
            dimension_semantics=("arbitrary",), vmem_limit_bytes=VMEM_LIMIT),
        name="mixer_step",
    )(x2, *weights, buf_a, buf_qkv, s0, *prev)
    return h2.reshape(x.shape), na, nq, ns


def _mlp(h, g2, w_up, w_down, gf, layer, final_norm, w_in_next=None):
    shape = h.shape
    h2 = h.reshape(-1, D_MODEL)
    rows = h2.shape[0]
    tm = min(MLP_TILE, rows)
    n_steps = rows // tm
    in_specs = [
        pl.BlockSpec((tm, D_MODEL), lambda i: (i, 0)),
        _resident((1, D_MODEL), layer),
        _resident((D_MODEL, D_FF), 0),
        _resident((D_FF, D_MODEL), 0),
        _resident((1, D_MODEL), 0),
    ]
    out_specs = [pl.BlockSpec((tm, D_MODEL), lambda i: (i, 0))]
    out_shape = [jax.ShapeDtypeStruct(h2.shape, F32)]
    operands = [h2, g2, w_up, w_down, gf]
    if w_in_next is not None:
        w_in, nxt = w_in_next
        blk = D_MODEL // n_steps
        in_specs.append(pl.BlockSpec((None, blk, IN_WIDTH), lambda i: (nxt, i, 0)))
        out_specs.append(pl.BlockSpec((None, blk, IN_WIDTH), lambda i: (0, i, 0)))
        out_shape.append(jax.ShapeDtypeStruct((1, D_MODEL, IN_WIDTH), BF16))
        operands.append(w_in)
    outs = pl.pallas_call(
        functools.partial(_mlp_kernel, final_norm=final_norm),
        grid=(n_steps,),
        in_specs=in_specs,
        out_specs=out_specs,
        out_shape=out_shape,
        compiler_params=pltpu.CompilerParams(
            dimension_semantics=("arbitrary",), vmem_limit_bytes=VMEM_LIMIT),
        name="mlp",
    )(*operands)
    return outs[0].reshape(shape), (outs[1] if w_in_next is not None else None)


def _head_rows(v):
    return jnp.pad(v.astype(F32), ((0, 0), (N_HEADS, LANES - 2 * N_HEADS)))[:, None, :]


def kernel(x_prompt, x_sample, state_conv_a, state_conv_qkv, state_delta, norm1_g, w_in, conv_a_w,
           w_a_out, conv_qkv_w, a_log, dt_bias, onorm_g, w_b_out, w_o, norm2_g, w_up, w_down,
           final_g):
    depth = w_in.shape[0]
    nb = x_prompt.shape[0]
    zero_a = jnp.zeros((nb, CONV_A_WIDTH - 1, W_A), F32)
    zero_qkv = jnp.zeros((nb, CONV_QKV_WIDTH - 1, W_QKV), F32)
    zero_s = jnp.zeros((nb, N_HEADS, HEAD_K, HEAD_V), F32)

    w_in_b = w_in[0:1].astype(BF16)
    w_small = jnp.concatenate(
        [w_in[:, :, OFF_G:OFF_G + LANES],
         jnp.pad(w_in[:, :, IN_WIDTH - GATE_SHIFT:], ((0, 0), (0, 0), (0, LANES - GATE_SHIFT)))],
        axis=2).astype(BF16)
    small_weights = (
        w_small, conv_a_w,
        jnp.roll(w_a_out.astype(BF16), GATE_SHIFT, axis=2), conv_qkv_w,
        _head_rows(a_log), _head_rows(dt_bias), onorm_g[:, None, :],
        jnp.roll(w_b_out.astype(BF16), GATE_SHIFT, axis=2),
        jnp.roll(w_o.astype(BF16), GATE_SHIFT, axis=1))
    g1 = norm1_g[:, None, :]
    g2 = norm2_g[:, None, :]
    gf = final_g.reshape(1, 1, D_MODEL)

    xp, xs = x_prompt, x_sample
    new_p = new_s = None
    for l in range(depth):
        last = l == depth - 1
        weights = (g1, w_in_b) + small_weights
        hp, *new_p, wu, wd = _mixer_stream(xp, zero_a, zero_qkv, zero_s, weights, l, depth, new_p,
                                           w_up, w_down)
        xp, w_in_b_next = _mlp(hp, g2, wu, wd, gf, l, last, None if last else (w_in, l + 1))
        hs, *new_s = _mixer_step(xs, state_conv_a, state_conv_qkv, state_delta, weights, l, new_s)
        xs, _ = _mlp(hs, g2, wu, wd, gf, l, last)
        w_in_b = w_in_b_next

    return (xp, xs, *new_p, *new_s)
```

```python
import functools

import jax
import jax.numpy as jnp
from jax import lax
from jax.experimental import pallas as pl
from jax.experimental.pallas import tpu as pltpu

F32 = jnp.float32
BF16 = jnp.bfloat16

D_MODEL = 1024
W_A = 512
N_HEADS = 4
HEAD_K = 128
HEAD_V = 128
W_QK = N_HEADS * HEAD_K
W_VV = N_HEADS * HEAD_V
W_QKV = 2 * W_QK + W_VV
CONV_A_WIDTH = 3
CONV_QKV_WIDTH = 4
D_FF = 4 * D_MODEL
EPS = 1e-6

LANES = 128
SUBLANES = 8
BLOCK_ROWS = 128

OFF_A = 0
OFF_QKV = OFF_A + 3 * W_A
OFF_Z = OFF_QKV + W_QKV
OFF_G = OFF_Z + W_VV
GATE_SHIFT = 2 * N_HEADS
IN_WIDTH = OFF_G + GATE_SHIFT + 2 * D_MODEL

PROMPT_CHUNK = 64
MIXER_TILE = 512
STEP_TILE = 256
MLP_TILE = 1024
FF_CHUNK = 1024
PIECE = 256
VMEM_LIMIT = 56 * 1024 * 1024


def _mm(a, b):
    return jnp.dot(a.astype(BF16), b.astype(BF16), preferred_element_type=F32)


def _mm_nt(a, b):
    return lax.dot_general(a.astype(BF16), b.astype(BF16), (((1,), (1,)), ((), ())),
                           preferred_element_type=F32)


def _mm_tn(a, b):
    return lax.dot_general(a.astype(BF16), b.astype(BF16), (((0,), (0,)), ((), ())),
                           preferred_element_type=F32)


def _rms(x, g):
    return x * lax.rsqrt(jnp.mean(x * x, axis=-1, keepdims=True) + EPS) * g


def _sigmoid(x):
    return 1.0 / (1.0 + jnp.exp(-x))


def _softplus(x):
    return jnp.maximum(x, 0.0) + jnp.log1p(jnp.exp(-jnp.abs(x)))


def _chunk_scan(x, pos, chunk, reverse):
    rows = x.shape[0]
    s = 1
    while s < chunk:
        if reverse:
            x = x + jnp.where(pos < chunk - s, pltpu.roll(x, rows - s, 0), 0.0)
        else:
            x = x + jnp.where(pos >= s, pltpu.roll(x, s, 0), 0.0)
        s *= 2
    return x


def _mixer_kernel(*refs, tm, chunk, streams, n_alias):
    streaming = streams > 0
    (x_ref, g1_ref, win_ref, wsmall_ref, caw_ref, waout_ref, cqw_ref, alog_ref, dtb_ref, og_ref,
     wbout_ref, wo_ref, sa_ref, sq_ref, s0_ref) = refs[:15]
    refs = refs[15:]
    if streaming:
        xnext_ref, wup_f32_ref, wdown_f32_ref = refs[:3]
        (h_ref, na_ref, nq_ref, ns_ref, wup_bf_ref, wdown_bf_ref,
         pad_a, pad_q, o_scr, xn_scr) = refs[3 + n_alias:]
    else:
        h_ref, na_ref, nq_ref, ns_ref, pad_a, pad_q, o_scr = refs[n_alias:]
    ts = tm // streams if streaming else tm
    nseq = tm // chunk
    row = lax.broadcasted_iota(jnp.int32, (tm, LANES), 0)
    pos = row & (chunk - 1)

    if streaming:
        @pl.when(pl.program_id(0) == 0)
        def _():
            ns_ref[...] = s0_ref[...]
            xn_scr[...] = _rms(x_ref[...].reshape(tm, D_MODEL), g1_ref[...]).astype(BF16)
            pad_a[...] = jnp.zeros(pad_a.shape, F32)
            pad_q[...] = jnp.zeros(pad_q.shape, F32)
            pad_a[:, SUBLANES - (CONV_A_WIDTH - 1):SUBLANES, :] = sa_ref[...]
            pad_q[:, SUBLANES - (CONV_QKV_WIDTH - 1):SUBLANES, :] = sq_ref[...]

    fillers = []

    def fill(n=1):
        for _ in range(min(n, len(fillers))):
            fillers.pop(0)()

    def queue_projection(lhs, w_ref, col0, width, out):
        def piece(j):
            def run():
                out[j] = _mm(lhs(), w_ref[:, col0 + j * PIECE:col0 + (j + 1) * PIECE])
            return run
        fillers.extend(piece(j) for j in range(width // PIECE))

    def causal_conv(pad, xin, w_ref, width, state_ref, new_ref, act):
        hw = width - 1
        taps = [w_ref[j:j + 1, :] for j in range(width)]
        if streaming:
            outs = []
            for b in range(streams):
                for r0 in range(b * ts, (b + 1) * ts, BLOCK_ROWS):
                    front = pad[b] if r0 == b * ts else xin[r0 - SUBLANES:r0]
                    slab = jnp.concatenate([front, xin[r0:r0 + BLOCK_ROWS]], axis=0)
                    y = xin[r0:r0 + BLOCK_ROWS] * taps[hw]
                    for s in range(1, width):
                        y = y + pltpu.roll(slab, s, 0)[SUBLANES:] * taps[hw - s]
                    outs.append(act(y))
                pad[b] = xin[(b + 1) * ts - SUBLANES:(b + 1) * ts]
                new_ref[b] = pad[b, SUBLANES - hw:SUBLANES, :]
            return jnp.concatenate(outs, axis=0)
        pad[0:tm, :] = xin
        pad[tm:2 * tm, :] = jnp.zeros((tm, xin.shape[1]), F32)
        for q in range(nseq):
            qp = (q - 1) % nseq
            pad[tm + qp * chunk + chunk - hw:tm + qp * chunk + chunk, :] = state_ref[q]
            new_ref[q] = pad[q * chunk + chunk - hw:q * chunk + chunk, :]
        hist = pad[tm:2 * tm, :]
        cpos = lax.broadcasted_iota(jnp.int32, (tm, 1), 0) & (chunk - 1)
        y = xin * taps[hw]
        for s in range(1, width):
            sh = jnp.where(cpos >= s, pltpu.roll(xin, s, 0), pltpu.roll(hist, s, 0))
            y = y + sh * taps[hw - s]
        return act(y)

    def wrap_lanes(piece, src):
        lane = lax.broadcasted_iota(jnp.int32, (tm, LANES), 1)
        return jnp.concatenate(
            [jnp.where(lane < GATE_SHIFT, src, piece[:, 0:LANES]), piece[:, LANES:]], axis=1)

    x = x_ref[...].reshape(tm, D_MODEL)
    xn = xn_scr[...] if streaming else _rms(x, g1_ref[...]).astype(BF16)

    pa_cols = [None] * (3 * W_A // PIECE)
    z_cols = [None] * (W_VV // PIECE)
    gate_cols = [None] * (2 * D_MODEL // PIECE)
    ya_cols = [None] * (D_MODEL // PIECE)
    queue_projection(lambda: xn, win_ref, OFF_A, 3 * W_A, pa_cols)
    queue_projection(lambda: xn, win_ref, OFF_Z, W_VV, z_cols)
    queue_projection(lambda: xn, win_ref, OFF_G, 2 * D_MODEL, gate_cols)

    p_q = _mm(xn, win_ref[:, OFF_QKV:OFF_QKV + W_QKV])
    small = _mm(xn, wsmall_ref[...])
    ba_raw = small[:, 0:LANES]
    gate_tail = small[:, LANES:]
    beta = _sigmoid(ba_raw)
    c = causal_conv(pad_q, p_q, cqw_ref, CONV_QKV_WIDTH, sq_ref, nq_ref, lambda y: y * _sigmoid(y))
    fill(len(pa_cols) + len(z_cols))

    g = -jnp.exp(alog_ref[...]) * _softplus(ba_raw + dtb_ref[...])
    gc = _chunk_scan(g, pos, chunk, reverse=False)
    g_after = _chunk_scan(g, pos, chunk, reverse=True) - g
    eg = jnp.exp(gc)
    e_after = jnp.exp(g_after)
    gc_t = gc.T

    ri = lax.broadcasted_iota(jnp.int32, (BLOCK_ROWS, BLOCK_ROWS), 0)
    ci = lax.broadcasted_iota(jnp.int32, (BLOCK_ROWS, BLOCK_ROWS), 1)
    m_incl = ((ri & -chunk) == (ci & -chunk)) & (ri >= ci)
    eye = jnp.where(ri == ci, 1.0, 0.0)
    m_levels = []
    size = 1
    while size < chunk:
        m_levels.append(((ri & -(2 * size)) == (ci & -(2 * size))) & ((ri & -size) > (ci & -size)))
        size *= 2

    pairs = [(rb, h) for rb in range(tm // BLOCK_ROWS) for h in range(N_HEADS)]
    q_g, k_t, rhs, p_blk, a_blk, t_inv = {}, {}, {}, {}, {}, {}
    for rb, h in pairs:
        rows = slice(rb * BLOCK_ROWS, (rb + 1) * BLOCK_ROWS)
        qh = c[rows, h * HEAD_K:(h + 1) * HEAD_K]
        kh = c[rows, W_QK + h * HEAD_K:W_QK + (h + 1) * HEAD_K]
        vh = c[rows, 2 * W_QK + h * HEAD_V:2 * W_QK + (h + 1) * HEAD_V]
        qh = qh * (lax.rsqrt(jnp.sum(qh * qh, axis=-1, keepdims=True) + EPS) * (HEAD_K ** -0.5))
        kh = kh * lax.rsqrt(jnp.sum(kh * kh, axis=-1, keepdims=True) + EPS)
        b_col = beta[rows, h:h + 1]
        hg = N_HEADS + h
        eg_col = eg[rows, hg:hg + 1]
        diff = gc[rows, hg:hg + 1] - gc_t[hg:hg + 1, rows]
        dec_incl = jnp.exp(jnp.where(m_incl, diff, -1e30))
        qk = _mm_nt(jnp.concatenate([qh, kh], axis=0), kh)
        p_blk[rb, h] = qk[0:BLOCK_ROWS] * dec_incl
        a_blk[rb, h] = qk[BLOCK_ROWS:] * dec_incl * b_col
        rhs[rb, h] = jnp.concatenate([vh * b_col, kh * (b_col * eg_col)], axis=1)
        q_g[rb, h] = qh * eg_col
        k_t[rb, h] = kh * e_after[rows, hg:hg + 1]
        t_inv[rb, h] = eye - jnp.where(m_levels[0], a_blk[rb, h], 0.0)

    n_gp = D_MODEL // PIECE
    fill(n_gp + 1)

    p_a = jnp.concatenate(pa_cols, axis=1)
    gated_u = causal_conv(pad_a, p_a[:, W_A:2 * W_A] * p_a[:, 2 * W_A:3 * W_A], caw_ref,
                          CONV_A_WIDTH, sa_ref, na_ref, lambda y: y) * p_a[:, 0:W_A]
    gated_u = gated_u.astype(BF16)
    queue_projection(lambda: gated_u, waout_ref, 0, D_MODEL, ya_cols)
    z = jnp.concatenate(z_cols, axis=1)
    z_act = z * _sigmoid(z)
    sig_a = _sigmoid(jnp.concatenate(
        [wrap_lanes(gate_cols[0], gate_cols[n_gp][:, 0:LANES])] + gate_cols[1:n_gp], axis=1))

    for m_off in m_levels[1:]:
        a_t = {p: _mm(jnp.where(m_off, a_blk[p], 0.0), t_inv[p]) for p in pairs}
        t_inv = {p: t_inv[p] - _mm(t_inv[p], a_t[p]) for p in pairs}
    w_vk = {p: _mm(t_inv[p], rhs[p]) for p in pairs}

    cpb = BLOCK_ROWS // chunk
    if streaming:
        cps = ts // chunk
        groups = [[(b * cps + j, h, b) for b in range(streams) for h in range(N_HEADS)]
                  for j in range(cps)]
    else:
        groups = [[(q, h, q) for q in range(nseq) for h in range(N_HEADS)]]
    for group in groups:
        s_old, xs, u_c = {}, {}, {}
        for q, h, slot in group:
            p, c0 = (q // cpb, h), (q % cpb) * chunk
            s_old[q, h] = ns_ref[slot, h] if streaming else s0_ref[slot, h]
            xs[q, h] = _mm(jnp.concatenate([w_vk[p][c0:c0 + chunk, HEAD_V:], q_g[p][c0:c0 + chunk]],
                                           axis=0), s_old[q, h])
        fill()
        for q, h, slot in group:
            p, c0 = (q // cpb, h), (q % cpb) * chunk
            u_c[q, h] = w_vk[p][c0:c0 + chunk, 0:HEAD_V] - xs[q, h][0:chunk]
            o_c = xs[q, h][chunk:] + _mm(p_blk[p][c0:c0 + chunk, c0:c0 + chunk], u_c[q, h])
            o_scr[q * chunk:(q + 1) * chunk, h * HEAD_V:(h + 1) * HEAD_V] = o_c
        for q, h, slot in group:
            p, c0 = (q // cpb, h), (q % cpb) * chunk
            g_last = eg[(q + 1) * chunk - 1:(q + 1) * chunk, N_HEADS + h:N_HEADS + h + 1]
            ns_ref[slot, h] = s_old[q, h] * g_last + _mm_tn(k_t[p][c0:c0 + chunk], u_c[q, h])
        fill()
    fill(len(fillers))

    og = og_ref[...]
    o_parts = []
    for h in range(N_HEADS):
        o_h = o_scr[:, h * HEAD_V:(h + 1) * HEAD_V]
        o_parts.append(_rms(o_h, og) * z_act[:, h * HEAD_V:(h + 1) * HEAD_V])
    y_b = _mm(jnp.concatenate(o_parts, axis=1), wbout_ref[...])

    sig_b = _sigmoid(jnp.concatenate(
        [wrap_lanes(gate_cols[n_gp], gate_tail)] + gate_cols[n_gp + 1:], axis=1))
    mixed = sig_a * jnp.concatenate(ya_cols, axis=1) + sig_b * y_b
    if streaming:
        xn_scr[...] = _rms(xnext_ref[...].reshape(tm, D_MODEL), g1_ref[...]).astype(BF16)
        wup_bf_ref[...] = wup_f32_ref[...].astype(BF16)
        wdown_bf_ref[...] = wdown_f32_ref[...].astype(BF16)
    h_ref[...] = (x + _mm(mixed, wo_ref[...])).reshape(h_ref.shape)


def _mlp_kernel(h_ref, g2_ref, wup_ref, wdown_ref, gf_ref, out_ref, *, final_norm):
    h = h_ref[...]
    hn = _rms(h, g2_ref[...]).astype(BF16)
    acc = h
    for j in range(D_FF // FF_CHUNK):
        up = jnp.dot(hn, wup_ref[:, j * FF_CHUNK:(j + 1) * FF_CHUNK], preferred_element_type=F32)
        act = jnp.maximum(up, 0.0)
        acc = acc + _mm(act * act, wdown_ref[j * FF_CHUNK:(j + 1) * FF_CHUNK, :])
    if final_norm:
        acc = _rms(acc, gf_ref[...])
    out_ref[...] = acc


def _resident(shape, layer):
    nd = len(shape)
    return pl.BlockSpec((None,) + shape, lambda *_: (layer,) + (0,) * nd,
                        pipeline_mode=pl.Buffered(1))


def _mixer_weight_specs(layer):
    return [
        _resident((1, D_MODEL), layer),
        _resident((D_MODEL, IN_WIDTH), layer),
        _resident((D_MODEL, 2 * LANES), layer),
        _resident((CONV_A_WIDTH, W_A), layer),
        _resident((W_A, D_MODEL), layer),
        _resident((CONV_QKV_WIDTH, W_QKV), layer),
        _resident((1, LANES), layer),
        _resident((1, LANES), layer),
        _resident((1, HEAD_V), layer),
        _resident((W_VV, D_MODEL), layer),
        _resident((D_MODEL, D_MODEL), layer),
    ]


def _alias_kwargs(n_inputs, stacked_prev):
    if stacked_prev is None:
        return [], [], {}
    specs = [pl.BlockSpec(memory_space=pl.ANY)] * len(stacked_prev)
    aliases = {n_inputs + k: 1 + k for k in range(len(stacked_prev))}
    return list(stacked_prev), specs, aliases


def _mixer_stream(x, buf_a, buf_qkv, s0, weights, layer, depth, stacked_prev, w_up, w_down):
    nb, seq, _ = x.shape
    tm = MIXER_TILE
    ts = tm // nb
    n_steps = seq // ts
    ff_blk = D_FF // n_steps
    whole3 = lambda i: (0, 0, 0)
    whole4 = lambda i: (0, 0, 0, 0)
    state_specs = [
        pl.BlockSpec((nb, CONV_A_WIDTH - 1, W_A), whole3),
        pl.BlockSpec((nb, CONV_QKV_WIDTH - 1, W_QKV), whole3),
        pl.BlockSpec((nb, N_HEADS, HEAD_K, HEAD_V), whole4),
    ]
    new_state_specs = [
        pl.BlockSpec((None, nb, CONV_A_WIDTH - 1, W_A), lambda i: (layer, 0, 0, 0)),
        pl.BlockSpec((None, nb, CONV_QKV_WIDTH - 1, W_QKV), lambda i: (layer, 0, 0, 0)),
        pl.BlockSpec((None, nb, N_HEADS, HEAD_K, HEAD_V), lambda i: (layer, 0, 0, 0, 0)),
    ]
    in_specs = ([pl.BlockSpec((nb, ts, D_MODEL), lambda i: (0, i, 0))]
                + _mixer_weight_specs(layer) + state_specs
                + [pl.BlockSpec((nb, ts, D_MODEL), lambda i: (0, jnp.minimum(i + 1, n_steps - 1), 0)),
                   pl.BlockSpec((None, D_MODEL, ff_blk), lambda i: (layer, 0, i)),
                   pl.BlockSpec((None, ff_blk, D_MODEL), lambda i: (layer, i, 0))])
    prev, prev_specs, aliases = _alias_kwargs(len(in_specs), stacked_prev)
    kern = functools.partial(_mixer_kernel, tm=tm, chunk=PROMPT_CHUNK, streams=nb, n_alias=len(prev))
    return pl.pallas_call(
        kern,
        grid=(n_steps,),
        in_specs=in_specs + prev_specs,
        out_specs=[pl.BlockSpec((nb, ts, D_MODEL), lambda i: (0, i, 0))] + new_state_specs + [
            pl.BlockSpec((None, D_MODEL, ff_blk), lambda i: (0, 0, i)),
            pl.BlockSpec((None, ff_blk, D_MODEL), lambda i: (0, i, 0)),
        ],
        out_shape=[
            jax.ShapeDtypeStruct(x.shape, F32),
            jax.ShapeDtypeStruct((depth,) + buf_a.shape, F32),
            jax.ShapeDtypeStruct((depth,) + buf_qkv.shape, F32),
            jax.ShapeDtypeStruct((depth,) + s0.shape, F32),
            jax.ShapeDtypeStruct((1, D_MODEL, D_FF), BF16),
            jax.ShapeDtypeStruct((1, D_FF, D_MODEL), BF16),
        ],
        input_output_aliases=aliases,
        scratch_shapes=[
            pltpu.VMEM((nb, SUBLANES, W_A), F32),
            pltpu.VMEM((nb, SUBLANES, W_QKV), F32),
            pltpu.VMEM((tm, W_VV), F32),
            pltpu.VMEM((tm, D_MODEL), BF16),
        ],
        compiler_params=pltpu.CompilerParams(
            dimension_semantics=("arbitrary",), vmem_limit_bytes=VMEM_LIMIT),
        name="mixer_stream",
    )(x, *weights, buf_a, buf_qkv, s0, x, w_up, w_down, *prev)


def _mixer_step(x, buf_a, buf_qkv, s0, weights, layer, stacked_prev):
    nb, seq, _ = x.shape
    tm = STEP_TILE
    nseq = tm // seq
    lay3 = lambda i: (layer, i, 0, 0)
    lay4 = lambda i: (layer, i, 0, 0, 0)
    x2 = x.reshape(nb * seq, D_MODEL)
    state_specs = [
        pl.BlockSpec((None, nseq, CONV_A_WIDTH - 1, W_A), lay3),
        pl.BlockSpec((None, nseq, CONV_QKV_WIDTH - 1, W_QKV), lay3),
        pl.BlockSpec((None, nseq, N_HEADS, HEAD_K, HEAD_V), lay4),
    ]
    in_specs = [pl.BlockSpec((tm, D_MODEL), lambda i: (i, 0))] + _mixer_weight_specs(layer) + state_specs
    prev, prev_specs, aliases = _alias_kwargs(len(in_specs), stacked_prev)
    kern = functools.partial(_mixer_kernel, tm=tm, chunk=seq, streams=0, n_alias=len(prev))
    h2, na, nq, ns = pl.pallas_call(
        kern,
        grid=(nb // nseq,),
        in_specs=in_specs + prev_specs,
        out_specs=[pl.BlockSpec((tm, D_MODEL), lambda i: (i, 0))] + state_specs,
        out_shape=[
            jax.ShapeDtypeStruct(x2.shape, F32),
            jax.ShapeDtypeStruct(buf_a.shape, F32),
            jax.ShapeDtypeStruct(buf_qkv.shape, F32),
            jax.ShapeDtypeStruct(s0.shape, F32),
        ],
        input_output_aliases=aliases,
        scratch_shapes=[
            pltpu.VMEM((2 * tm, W_A), F32),
            pltpu.VMEM((2 * tm, W_QKV), F32),
            pltpu.VMEM((tm, W_VV), F32),
        ],
        compiler_params=pltpu.CompilerParams(
            dimension_semantics=("arbitrary",), vmem_limit_bytes=VMEM_LIMIT),
        name="mixer_step",
    )(x2, *weights, buf_a, buf_qkv, s0, *prev)
    return h2.reshape(x.shape), na, nq, ns


def _mlp(h, g2, w_up, w_down, gf, layer, final_norm):
    shape = h.shape
    h2 = h.reshape(-1, D_MODEL)
    rows = h2.shape[0]
    tm = min(MLP_TILE, rows)
    out = pl.pallas_call(
        functools.partial(_mlp_kernel, final_norm=final_norm),
        grid=(rows // tm,),
        in_specs=[
            pl.BlockSpec((tm, D_MODEL), lambda i: (i, 0)),
            _resident((1, D_MODEL), layer),
            _resident((D_MODEL, D_FF), 0),
            _resident((D_FF, D_MODEL), 0),
            _resident((1, D_MODEL), 0),
        ],
        out_specs=pl.BlockSpec((tm, D_MODEL), lambda i: (i, 0)),
        out_shape=jax.ShapeDtypeStruct(h2.shape, F32),
        compiler_params=pltpu.CompilerParams(
            dimension_semantics=("arbitrary",), vmem_limit_bytes=VMEM_LIMIT),
        name="mlp",
    )(h2, g2, w_up, w_down, gf)
    return out.reshape(shape)


def _head_rows(v):
    return jnp.pad(v.astype(F32), ((0, 0), (N_HEADS, LANES - 2 * N_HEADS)))[:, None, :]


def kernel(x_prompt, x_sample, state_conv_a, state_conv_qkv, state_delta, norm1_g, w_in, conv_a_w,
           w_a_out, conv_qkv_w, a_log, dt_bias, onorm_g, w_b_out, w_o, norm2_g, w_up, w_down,
           final_g):
    depth = w_in.shape[0]
    nb = x_prompt.shape[0]
    zero_a = jnp.zeros((nb, CONV_A_WIDTH - 1, W_A), F32)
    zero_qkv = jnp.zeros((nb, CONV_QKV_WIDTH - 1, W_QKV), F32)
    zero_s = jnp.zeros((nb, N_HEADS, HEAD_K, HEAD_V), F32)

    w_in_b = w_in.astype(BF16)
    w_small = jnp.concatenate(
        [w_in_b[:, :, OFF_G:OFF_G + LANES],
         jnp.pad(w_in_b[:, :, IN_WIDTH - GATE_SHIFT:], ((0, 0), (0, 0), (0, LANES - GATE_SHIFT)))], axis=2)
    weights = (
        norm1_g[:, None, :], w_in_b, w_small, conv_a_w,
        jnp.roll(w_a_out.astype(BF16), GATE_SHIFT, axis=2), conv_qkv_w,
        _head_rows(a_log), _head_rows(dt_bias), onorm_g[:, None, :],
        jnp.roll(w_b_out.astype(BF16), GATE_SHIFT, axis=2),
        jnp.roll(w_o.astype(BF16), GATE_SHIFT, axis=1))
    g2 = norm2_g[:, None, :]
    gf = final_g.reshape(1, 1, D_MODEL)

    xp, xs = x_prompt, x_sample
    new_p = new_s = None
    for l in range(depth):
        last = l == depth - 1
        hp, *new_p, wu, wd = _mixer_stream(xp, zero_a, zero_qkv, zero_s, weights, l, depth, new_p,
                                           w_up, w_down)
        xp = _mlp(hp, g2, wu, wd, gf, l, last)
        hs, *new_s = _mixer_step(xs, state_conv_a, state_conv_qkv, state_delta, weights, l, new_s)
        xs = _mlp(hs, g2, wu, wd, gf, l, last)

    return (xp, xs, *new_p, *new_s)
```

```python
import functools

import jax
import jax.numpy as jnp
from jax import lax
from jax.experimental import pallas as pl
from jax.experimental.pallas import tpu as pltpu

F32 = jnp.float32
BF16 = jnp.bfloat16

D_MODEL = 1024
W_A = 512
N_HEADS = 4
HEAD_K = 128
HEAD_V = 128
W_QK = N_HEADS * HEAD_K
W_VV = N_HEADS * HEAD_V
W_QKV = 2 * W_QK + W_VV
CONV_A_WIDTH = 3
CONV_QKV_WIDTH = 4
D_FF = 4 * D_MODEL
EPS = 1e-6

LANES = 128
SUBLANES = 8
BLOCK_ROWS = 128

OFF_A = 0
OFF_QKV = OFF_A + 3 * W_A
OFF_Z = OFF_QKV + W_QKV
OFF_G = OFF_Z + W_VV
GATE_SHIFT = 2 * N_HEADS
IN_WIDTH = OFF_G + GATE_SHIFT + 2 * D_MODEL

PROMPT_CHUNK = 64
MIXER_TILE = 512
STEP_TILE = 256
MLP_TILE = 1024
FF_CHUNK = 1024
PIECE = 256
VMEM_LIMIT = 56 * 1024 * 1024


def _mm(a, b):
    return jnp.dot(a.astype(BF16), b.astype(BF16), preferred_element_type=F32)


def _mm_nt(a, b):
    return lax.dot_general(a.astype(BF16), b.astype(BF16), (((1,), (1,)), ((), ())),
                           preferred_element_type=F32)


def _mm_tn(a, b):
    return lax.dot_general(a.astype(BF16), b.astype(BF16), (((0,), (0,)), ((), ())),
                           preferred_element_type=F32)


def _rms(x, g):
    return x * lax.rsqrt(jnp.mean(x * x, axis=-1, keepdims=True) + EPS) * g


def _sigmoid(x):
    return 1.0 / (1.0 + jnp.exp(-x))


def _softplus(x):
    return jnp.maximum(x, 0.0) + jnp.log1p(jnp.exp(-jnp.abs(x)))


def _chunk_scan(x, pos, chunk, reverse):
    rows = x.shape[0]
    s = 1
    while s < chunk:
        if reverse:
            x = x + jnp.where(pos < chunk - s, pltpu.roll(x, rows - s, 0), 0.0)
        else:
            x = x + jnp.where(pos >= s, pltpu.roll(x, s, 0), 0.0)
        s *= 2
    return x


def _mixer_kernel(*refs, tm, chunk, streams, n_alias):
    streaming = streams > 0
    (x_ref, g1_ref, win_ref, wsmall_ref, caw_ref, waout_ref, cqw_ref, alog_ref, dtb_ref, og_ref,
     wbout_ref, wo_ref, sa_ref, sq_ref, s0_ref) = refs[:15]
    refs = refs[15:]
    if streaming:
        xnext_ref, wup_f32_ref, wdown_f32_ref = refs[:3]
        (h_ref, na_ref, nq_ref, ns_ref, wup_bf_ref, wdown_bf_ref,
         pad_a, pad_q, o_scr, xn_scr) = refs[3 + n_alias:]
    else:
        h_ref, na_ref, nq_ref, ns_ref, pad_a, pad_q, o_scr = refs[n_alias:]
    ts = tm // streams if streaming else tm
    nseq = tm // chunk
    row = lax.broadcasted_iota(jnp.int32, (tm, LANES), 0)
    pos = row & (chunk - 1)

    if streaming:
        @pl.when(pl.program_id(0) == 0)
        def _():
            ns_ref[...] = s0_ref[...]
            xn_scr[...] = _rms(x_ref[...].reshape(tm, D_MODEL), g1_ref[...]).astype(BF16)
            pad_a[...] = jnp.zeros(pad_a.shape, F32)
            pad_q[...] = jnp.zeros(pad_q.shape, F32)
            pad_a[:, SUBLANES - (CONV_A_WIDTH - 1):SUBLANES, :] = sa_ref[...]
            pad_q[:, SUBLANES - (CONV_QKV_WIDTH - 1):SUBLANES, :] = sq_ref[...]

    fillers = []

    def fill(n=1):
        for _ in range(min(n, len(fillers))):
            fillers.pop(0)()

    def queue_projection(lhs, w_ref, col0, width, out):
        def piece(j):
            def run():
                out[j] = _mm(lhs(), w_ref[:, col0 + j * PIECE:col0 + (j + 1) * PIECE])
            return run
        fillers.extend(piece(j) for j in range(width // PIECE))

    def causal_conv(pad, xin, w_ref, width, state_ref, new_ref, act):
        hw = width - 1
        taps = [w_ref[j:j + 1, :] for j in range(width)]
        if streaming:
            outs = []
            for b in range(streams):
                for r0 in range(b * ts, (b + 1) * ts, BLOCK_ROWS):
                    front = pad[b] if r0 == b * ts else xin[r0 - SUBLANES:r0]
                    slab = jnp.concatenate([front, xin[r0:r0 + BLOCK_ROWS]], axis=0)
                    y = xin[r0:r0 + BLOCK_ROWS] * taps[hw]
                    for s in range(1, width):
                        y = y + pltpu.roll(slab, s, 0)[SUBLANES:] * taps[hw - s]
                    outs.append(act(y))
                pad[b] = xin[(b + 1) * ts - SUBLANES:(b + 1) * ts]
                new_ref[b] = pad[b, SUBLANES - hw:SUBLANES, :]
            return jnp.concatenate(outs, axis=0)
        pad[0:tm, :] = xin
        pad[tm:2 * tm, :] = jnp.zeros((tm, xin.shape[1]), F32)
        for q in range(nseq):
            qp = (q - 1) % nseq
            pad[tm + qp * chunk + chunk - hw:tm + qp * chunk + chunk, :] = state_ref[q]
            new_ref[q] = pad[q * chunk + chunk - hw:q * chunk + chunk, :]
        hist = pad[tm:2 * tm, :]
        cpos = lax.broadcasted_iota(jnp.int32, (tm, 1), 0) & (chunk - 1)
        y = xin * taps[hw]
        for s in range(1, width):
            sh = jnp.where(cpos >= s, pltpu.roll(xin, s, 0), pltpu.roll(hist, s, 0))
            y = y + sh * taps[hw - s]
        return act(y)

    def wrap_lanes(piece, src):
        lane = lax.broadcasted_iota(jnp.int32, (tm, LANES), 1)
        return jnp.concatenate(
            [jnp.where(lane < GATE_SHIFT, src, piece[:, 0:LANES]), piece[:, LANES:]], axis=1)

    x = x_ref[...].reshape(tm, D_MODEL)
    xn = xn_scr[...] if streaming else _rms(x, g1_ref[...]).astype(BF16)

    pa_cols = [None] * (3 * W_A // PIECE)
    z_cols = [None] * (W_VV // PIECE)
    gate_cols = [None] * (2 * D_MODEL // PIECE)
    ya_cols = [None] * (D_MODEL // PIECE)
    queue_projection(lambda: xn, win_ref, OFF_A, 3 * W_A, pa_cols)
    queue_projection(lambda: xn, win_ref, OFF_Z, W_VV, z_cols)
    queue_projection(lambda: xn, win_ref, OFF_G, 2 * D_MODEL, gate_cols)

    p_q = _mm(xn, win_ref[:, OFF_QKV:OFF_QKV + W_QKV])
    small = _mm(xn, wsmall_ref[...])
    ba_raw = small[:, 0:LANES]
    gate_tail = small[:, LANES:]
    beta = _sigmoid(ba_raw)
    c = causal_conv(pad_q, p_q, cqw_ref, CONV_QKV_WIDTH, sq_ref, nq_ref, lambda y: y * _sigmoid(y))
    fill(len(pa_cols) + len(z_cols))

    g = -jnp.exp(alog_ref[...]) * _softplus(ba_raw + dtb_ref[...])
    gc = _chunk_scan(g, pos, chunk, reverse=False)
    g_after = _chunk_scan(g, pos, chunk, reverse=True) - g
    eg = jnp.exp(gc)
    e_after = jnp.exp(g_after)
    gc_t = gc.T

    ri = lax.broadcasted_iota(jnp.int32, (BLOCK_ROWS, BLOCK_ROWS), 0)
    ci = lax.broadcasted_iota(jnp.int32, (BLOCK_ROWS, BLOCK_ROWS), 1)
    m_incl = ((ri & -chunk) == (ci & -chunk)) & (ri >= ci)
    eye = jnp.where(ri == ci, 1.0, 0.0)
    m_levels = []
    size = 1
    while size < chunk:
        m_levels.append(((ri & -(2 * size)) == (ci & -(2 * size))) & ((ri & -size) > (ci & -size)))
        size *= 2

    pairs = [(rb, h) for rb in range(tm // BLOCK_ROWS) for h in range(N_HEADS)]
    q_g, k_t, rhs, p_blk, a_blk, t_inv = {}, {}, {}, {}, {}, {}
    for rb, h in pairs:
        rows = slice(rb * BLOCK_ROWS, (rb + 1) * BLOCK_ROWS)
        qh = c[rows, h * HEAD_K:(h + 1) * HEAD_K]
        kh = c[rows, W_QK + h * HEAD_K:W_QK + (h + 1) * HEAD_K]
        vh = c[rows, 2 * W_QK + h * HEAD_V:2 * W_QK + (h + 1) * HEAD_V]
        qh = qh * (lax.rsqrt(jnp.sum(qh * qh, axis=-1, keepdims=True) + EPS) * (HEAD_K ** -0.5))
        kh = kh * lax.rsqrt(jnp.sum(kh * kh, axis=-1, keepdims=True) + EPS)
        b_col = beta[rows, h:h + 1]
        hg = N_HEADS + h
        eg_col = eg[rows, hg:hg + 1]
        diff = gc[rows, hg:hg + 1] - gc_t[hg:hg + 1, rows]
        dec_incl = jnp.exp(jnp.where(m_incl, diff, -1e30))
        qk = _mm_nt(jnp.concatenate([qh, kh], axis=0), kh)
        p_blk[rb, h] = qk[0:BLOCK_ROWS] * dec_incl
        a_blk[rb, h] = qk[BLOCK_ROWS:] * dec_incl * b_col
        rhs[rb, h] = jnp.concatenate([vh * b_col, kh * (b_col * eg_col)], axis=1)
        q_g[rb, h] = qh * eg_col
        k_t[rb, h] = kh * e_after[rows, hg:hg + 1]
        t_inv[rb, h] = eye - jnp.where(m_levels[0], a_blk[rb, h], 0.0)

    n_gp = D_MODEL // PIECE
    fill(n_gp + 1)

    p_a = jnp.concatenate(pa_cols, axis=1)
    gated_u = causal_conv(pad_a, p_a[:, W_A:2 * W_A] * p_a[:, 2 * W_A:3 * W_A], caw_ref,
                          CONV_A_WIDTH, sa_ref, na_ref, lambda y: y) * p_a[:, 0:W_A]
    gated_u = gated_u.astype(BF16)
    queue_projection(lambda: gated_u, waout_ref, 0, D_MODEL, ya_cols)
    z = jnp.concatenate(z_cols, axis=1)
    z_act = z * _sigmoid(z)
    sig_a = _sigmoid(jnp.concatenate(
        [wrap_lanes(gate_cols[0], gate_cols[n_gp][:, 0:LANES])] + gate_cols[1:n_gp], axis=1))

    for m_off in m_levels[1:]:
        a_t = {p: _mm(jnp.where(m_off, a_blk[p], 0.0), t_inv[p]) for p in pairs}
        t_inv = {p: t_inv[p] - _mm(t_inv[p], a_t[p]) for p in pairs}
    w_vk = {p: _mm(t_inv[p], rhs[p]) for p in pairs}

    cpb = BLOCK_ROWS // chunk
    if streaming:
        cps = ts // chunk
        groups = [[(b * cps + j, h, b) for b in range(streams) for h in range(N_HEADS)]
                  for j in range(cps)]
    else:
        groups = [[(q, h, q) for q in range(nseq) for h in range(N_HEADS)]]
    for group in groups:
        s_old, xs, u_c = {}, {}, {}
        for q, h, slot in group:
            p, c0 = (q // cpb, h), (q % cpb) * chunk
            s_old[q, h] = ns_ref[slot, h] if streaming else s0_ref[slot, h]
            xs[q, h] = _mm(jnp.concatenate([w_vk[p][c0:c0 + chunk, HEAD_V:], q_g[p][c0:c0 + chunk]],
                                           axis=0), s_old[q, h])
        fill()
        for q, h, slot in group:
            p, c0 = (q // cpb, h), (q % cpb) * chunk
            u_c[q, h] = w_vk[p][c0:c0 + chunk, 0:HEAD_V] - xs[q, h][0:chunk]
            o_c = xs[q, h][chunk:] + _mm(p_blk[p][c0:c0 + chunk, c0:c0 + chunk], u_c[q, h])
            o_scr[q * chunk:(q + 1) * chunk, h * HEAD_V:(h + 1) * HEAD_V] = o_c
        for q, h, slot in group:
            p, c0 = (q // cpb, h), (q % cpb) * chunk
            g_last = eg[(q + 1) * chunk - 1:(q + 1) * chunk, N_HEADS + h:N_HEADS + h + 1]
            ns_ref[slot, h] = s_old[q, h] * g_last + _mm_tn(k_t[p][c0:c0 + chunk], u_c[q, h])
        fill()
    fill(len(fillers))

    og = og_ref[...]
    o_parts = []
    for h in range(N_HEADS):
        o_h = o_scr[:, h * HEAD_V:(h + 1) * HEAD_V]
        o_parts.append(_rms(o_h, og) * z_act[:, h * HEAD_V:(h + 1) * HEAD_V])
    y_b = _mm(jnp.concatenate(o_parts, axis=1), wbout_ref[...])

    sig_b = _sigmoid(jnp.concatenate(
        [wrap_lanes(gate_cols[n_gp], gate_tail)] + gate_cols[n_gp + 1:], axis=1))
    mixed = sig_a * jnp.concatenate(ya_cols, axis=1) + sig_b * y_b
    if streaming:
        xn_scr[...] = _rms(xnext_ref[...].reshape(tm, D_MODEL), g1_ref[...]).astype(BF16)
        wup_bf_ref[...] = wup_f32_ref[...].astype(BF16)
        wdown_bf_ref[...] = wdown_f32_ref[...].astype(BF16)
    h_ref[...] = (x + _mm(mixed, wo_ref[...])).reshape(h_ref.shape)


def _mlp_kernel(ha_ref, hb_ref, g2_ref, wup_ref, wdown_ref, gf_ref, outa_ref, outb_ref, *,
                final_norm, n_a):
    def body(h_ref, out_ref):
        h = h_ref[...]
        hn = _rms(h, g2_ref[...]).astype(BF16)
        acc = h
        for j in range(D_FF // FF_CHUNK):
            up = jnp.dot(hn, wup_ref[:, j * FF_CHUNK:(j + 1) * FF_CHUNK], preferred_element_type=F32)
            act = jnp.maximum(up, 0.0)
            acc = acc + _mm(act * act, wdown_ref[j * FF_CHUNK:(j + 1) * FF_CHUNK, :])
        if final_norm:
            acc = _rms(acc, gf_ref[...])
        out_ref[...] = acc

    step = pl.program_id(0)
    pl.when(step < n_a)(functools.partial(body, ha_ref, outa_ref))
    pl.when(step >= n_a)(functools.partial(body, hb_ref, outb_ref))


def _resident(shape, layer):
    nd = len(shape)
    return pl.BlockSpec((None,) + shape, lambda *_: (layer,) + (0,) * nd,
                        pipeline_mode=pl.Buffered(1))


def _mixer_weight_specs(layer):
    return [
        _resident((1, D_MODEL), layer),
        _resident((D_MODEL, IN_WIDTH), layer),
        _resident((D_MODEL, 2 * LANES), layer),
        _resident((CONV_A_WIDTH, W_A), layer),
        _resident((W_A, D_MODEL), layer),
        _resident((CONV_QKV_WIDTH, W_QKV), layer),
        _resident((1, LANES), layer),
        _resident((1, LANES), layer),
        _resident((1, HEAD_V), layer),
        _resident((W_VV, D_MODEL), layer),
        _resident((D_MODEL, D_MODEL), layer),
    ]


def _alias_kwargs(n_inputs, stacked_prev):
    if stacked_prev is None:
        return [], [], {}
    specs = [pl.BlockSpec(memory_space=pl.ANY)] * len(stacked_prev)
    aliases = {n_inputs + k: 1 + k for k in range(len(stacked_prev))}
    return list(stacked_prev), specs, aliases


def _mixer_stream(x, buf_a, buf_qkv, s0, weights, layer, depth, stacked_prev, w_up, w_down):
    nb, seq, _ = x.shape
    tm = MIXER_TILE
    ts = tm // nb
    n_steps = seq // ts
    ff_blk = D_FF // n_steps
    whole3 = lambda i: (0, 0, 0)
    whole4 = lambda i: (0, 0, 0, 0)
    state_specs = [
        pl.BlockSpec((nb, CONV_A_WIDTH - 1, W_A), whole3),
        pl.BlockSpec((nb, CONV_QKV_WIDTH - 1, W_QKV), whole3),
        pl.BlockSpec((nb, N_HEADS, HEAD_K, HEAD_V), whole4),
    ]
    new_state_specs = [
        pl.BlockSpec((None, nb, CONV_A_WIDTH - 1, W_A), lambda i: (layer, 0, 0, 0)),
        pl.BlockSpec((None, nb, CONV_QKV_WIDTH - 1, W_QKV), lambda i: (layer, 0, 0, 0)),
        pl.BlockSpec((None, nb, N_HEADS, HEAD_K, HEAD_V), lambda i: (layer, 0, 0, 0, 0)),
    ]
    in_specs = ([pl.BlockSpec((nb, ts, D_MODEL), lambda i: (0, i, 0))]
                + _mixer_weight_specs(layer) + state_specs
                + [pl.BlockSpec((nb, ts, D_MODEL), lambda i: (0, jnp.minimum(i + 1, n_steps - 1), 0)),
                   pl.BlockSpec((None, D_MODEL, ff_blk), lambda i: (layer, 0, i)),
                   pl.BlockSpec((None, ff_blk, D_MODEL), lambda i: (layer, i, 0))])
    prev, prev_specs, aliases = _alias_kwargs(len(in_specs), stacked_prev)
    kern = functools.partial(_mixer_kernel, tm=tm, chunk=PROMPT_CHUNK, streams=nb, n_alias=len(prev))
    return pl.pallas_call(
        kern,
        grid=(n_steps,),
        in_specs=in_specs + prev_specs,
        out_specs=[pl.BlockSpec((nb, ts, D_MODEL), lambda i: (0, i, 0))] + new_state_specs + [
            pl.BlockSpec((None, D_MODEL, ff_blk), lambda i: (0, 0, i)),
            pl.BlockSpec((None, ff_blk, D_MODEL), lambda i: (0, i, 0)),
        ],
        out_shape=[
            jax.ShapeDtypeStruct(x.shape, F32),
            jax.ShapeDtypeStruct((depth,) + buf_a.shape, F32),
            jax.ShapeDtypeStruct((depth,) + buf_qkv.shape, F32),
            jax.ShapeDtypeStruct((depth,) + s0.shape, F32),
            jax.ShapeDtypeStruct((1, D_MODEL, D_FF), BF16),
            jax.ShapeDtypeStruct((1, D_FF, D_MODEL), BF16),
        ],
        input_output_aliases=aliases,
        scratch_shapes=[
            pltpu.VMEM((nb, SUBLANES, W_A), F32),
            pltpu.VMEM((nb, SUBLANES, W_QKV), F32),
            pltpu.VMEM((tm, W_VV), F32),
            pltpu.VMEM((tm, D_MODEL), BF16),
        ],
        compiler_params=pltpu.CompilerParams(
            dimension_semantics=("arbitrary",), vmem_limit_bytes=VMEM_LIMIT),
        name="mixer_stream",
    )(x, *weights, buf_a, buf_qkv, s0, x, w_up, w_down, *prev)


def _mixer_step(x, buf_a, buf_qkv, s0, weights, layer, stacked_prev):
    nb, seq, _ = x.shape
    tm = STEP_TILE
    nseq = tm // seq
    lay3 = lambda i: (layer, i, 0, 0)
    lay4 = lambda i: (layer, i, 0, 0, 0)
    x2 = x.reshape(nb * seq, D_MODEL)
    state_specs = [
        pl.BlockSpec((None, nseq, CONV_A_WIDTH - 1, W_A), lay3),
        pl.BlockSpec((None, nseq, CONV_QKV_WIDTH - 1, W_QKV), lay3),
        pl.BlockSpec((None, nseq, N_HEADS, HEAD_K, HEAD_V), lay4),
    ]
    in_specs = [pl.BlockSpec((tm, D_MODEL), lambda i: (i, 0))] + _mixer_weight_specs(layer) + state_specs
    prev, prev_specs, aliases = _alias_kwargs(len(in_specs), stacked_prev)
    kern = functools.partial(_mixer_kernel, tm=tm, chunk=seq, streams=0, n_alias=len(prev))
    h2, na, nq, ns = pl.pallas_call(
        kern,
        grid=(nb // nseq,),
        in_specs=in_specs + prev_specs,
        out_specs=[pl.BlockSpec((tm, D_MODEL), lambda i: (i, 0))] + state_specs,
        out_shape=[
            jax.ShapeDtypeStruct(x2.shape, F32),
            jax.ShapeDtypeStruct(buf_a.shape, F32),
            jax.ShapeDtypeStruct(buf_qkv.shape, F32),
            jax.ShapeDtypeStruct(s0.shape, F32),
        ],
        input_output_aliases=aliases,
        scratch_shapes=[
            pltpu.VMEM((2 * tm, W_A), F32),
            pltpu.VMEM((2 * tm, W_QKV), F32),
            pltpu.VMEM((tm, W_VV), F32),
        ],
        compiler_params=pltpu.CompilerParams(
            dimension_semantics=("arbitrary",), vmem_limit_bytes=VMEM_LIMIT),
        name="mixer_step",
    )(x2, *weights, buf_a, buf_qkv, s0, *prev)
    return h2.reshape(x.shape), na, nq, ns


def _mlp(h_a, h_b, g2, w_up, w_down, gf, layer, final_norm):
    a2 = h_a.reshape(-1, D_MODEL)
    b2 = h_b.reshape(-1, D_MODEL)
    ta = min(MLP_TILE, a2.shape[0])
    tb = min(MLP_TILE, b2.shape[0])
    n_a = a2.shape[0] // ta
    n_b = b2.shape[0] // tb
    a_idx = lambda i: (jnp.minimum(i, n_a - 1), 0)
    b_idx = lambda i: (jnp.maximum(i - n_a, 0), 0)
    out_a, out_b = pl.pallas_call(
        functools.partial(_mlp_kernel, final_norm=final_norm, n_a=n_a),
        grid=(n_a + n_b,),
        in_specs=[
            pl.BlockSpec((ta, D_MODEL), a_idx),
            pl.BlockSpec((tb, D_MODEL), b_idx),
            _resident((1, D_MODEL), layer),
            _resident((D_MODEL, D_FF), 0),
            _resident((D_FF, D_MODEL), 0),
            _resident((1, D_MODEL), 0),
        ],
        out_specs=[pl.BlockSpec((ta, D_MODEL), a_idx), pl.BlockSpec((tb, D_MODEL), b_idx)],
        out_shape=[jax.ShapeDtypeStruct(a2.shape, F32), jax.ShapeDtypeStruct(b2.shape, F32)],
        compiler_params=pltpu.CompilerParams(
            dimension_semantics=("arbitrary",), vmem_limit_bytes=VMEM_LIMIT),
        name="mlp",
    )(a2, b2, g2, w_up, w_down, gf)
    return out_a.reshape(h_a.shape), out_b.reshape(h_b.shape)


def _head_rows(v):
    return jnp.pad(v.astype(F32), ((0, 0), (N_HEADS, LANES - 2 * N_HEADS)))[:, None, :]


def kernel(x_prompt, x_sample, state_conv_a, state_conv_qkv, state_delta, norm1_g, w_in, conv_a_w,
           w_a_out, conv_qkv_w, a_log, dt_bias, onorm_g, w_b_out, w_o, norm2_g, w_up, w_down,
           final_g):
    depth = w_in.shape[0]
    nb = x_prompt.shape[0]
    zero_a = jnp.zeros((nb, CONV_A_WIDTH - 1, W_A), F32)
    zero_qkv = jnp.zeros((nb, CONV_QKV_WIDTH - 1, W_QKV), F32)
    zero_s = jnp.zeros((nb, N_HEADS, HEAD_K, HEAD_V), F32)

    w_in_b = w_in.astype(BF16)
    w_small = jnp.concatenate(
        [w_in_b[:, :, OFF_G:OFF_G + LANES],
         jnp.pad(w_in_b[:, :, IN_WIDTH - GATE_SHIFT:], ((0, 0), (0, 0), (0, LANES - GATE_SHIFT)))], axis=2)
    weights = (
        norm1_g[:, None, :], w_in_b, w_small, conv_a_w,
        jnp.roll(w_a_out.astype(BF16), GATE_SHIFT, axis=2), conv_qkv_w,
        _head_rows(a_log), _head_rows(dt_bias), onorm_g[:, None, :],
        jnp.roll(w_b_out.astype(BF16), GATE_SHIFT, axis=2),
        jnp.roll(w_o.astype(BF16), GATE_SHIFT, axis=1))
    g2 = norm2_g[:, None, :]
    gf = final_g.reshape(1, 1, D_MODEL)

    xp, xs = x_prompt, x_sample
    new_p = new_s = None
    for l in range(depth):
        last = l == depth - 1
        hp, *new_p, wu, wd = _mixer_stream(xp, zero_a, zero_qkv, zero_s, weights, l, depth, new_p,
                                           w_up, w_down)
        hs, *new_s = _mixer_step(xs, state_conv_a, state_conv_qkv, state_delta, weights, l, new_s)
        xp, xs = _mlp(hp, hs, g2, wu, wd, gf, l, last)

    return (xp, xs, *new_p, *new_s)
```

```python
import functools

import jax
import jax.numpy as jnp
from jax import lax
from jax.experimental import pallas as pl
from jax.experimental.pallas import tpu as pltpu

F32 = jnp.float32
BF16 = jnp.bfloat16

D_MODEL = 1024
W_A = 512
N_HEADS = 4
HEAD_K = 128
HEAD_V = 128
W_QK = N_HEADS * HEAD_K
W_VV = N_HEADS * HEAD_V
W_QKV = 2 * W_QK + W_VV
CONV_A_WIDTH = 3
CONV_QKV_WIDTH = 4
D_FF = 4 * D_MODEL
EPS = 1e-6

LANES = 128
SUBLANES = 8
BLOCK_ROWS = 128

OFF_A = 0
OFF_QKV = OFF_A + 3 * W_A
OFF_Z = OFF_QKV + W_QKV
OFF_G = OFF_Z + W_VV
GATE_SHIFT = 2 * N_HEADS
IN_WIDTH = OFF_G + GATE_SHIFT + 2 * D_MODEL

PROMPT_CHUNK = 64
MIXER_TILE = 512
STEP_TILE = 256
MLP_TILE = 1024
FF_CHUNK = 1024
PIECE = 256
VMEM_LIMIT = 56 * 1024 * 1024


def _mm(a, b):
    return jnp.dot(a.astype(BF16), b.astype(BF16), preferred_element_type=F32)


def _mm_nt(a, b):
    return lax.dot_general(a.astype(BF16), b.astype(BF16), (((1,), (1,)), ((), ())),
                           preferred_element_type=F32)


def _mm_tn(a, b):
    return lax.dot_general(a.astype(BF16), b.astype(BF16), (((0,), (0,)), ((), ())),
                           preferred_element_type=F32)


def _rms(x, g):
    return x * lax.rsqrt(jnp.mean(x * x, axis=-1, keepdims=True) + EPS) * g


def _sigmoid(x):
    return 1.0 / (1.0 + jnp.exp(-x))


def _softplus(x):
    return jnp.maximum(x, 0.0) + jnp.log1p(jnp.exp(-jnp.abs(x)))


def _chunk_scan(x, pos, chunk, reverse):
    rows = x.shape[0]
    s = 1
    while s < chunk:
        if reverse:
            x = x + jnp.where(pos < chunk - s, pltpu.roll(x, rows - s, 0), 0.0)
        else:
            x = x + jnp.where(pos >= s, pltpu.roll(x, s, 0), 0.0)
        s *= 2
    return x


def _mixer_kernel(*refs, tm, chunk, streams, n_alias):
    streaming = streams > 0
    (x_ref, g1_ref, win_ref, wsmall_ref, caw_ref, waout_ref, cqw_ref, alog_ref, dtb_ref, og_ref,
     wbout_ref, wo_ref, sa_ref, sq_ref, s0_ref) = refs[:15]
    refs = refs[15:]
    if streaming:
        xnext_ref, wup_f32_ref, wdown_f32_ref = refs[:3]
        (h_ref, na_ref, nq_ref, ns_ref, wup_bf_ref, wdown_bf_ref,
         pad_a, pad_q, o_scr, xn_scr) = refs[3 + n_alias:]
    else:
        h_ref, na_ref, nq_ref, ns_ref, pad_a, pad_q, o_scr = refs[n_alias:]
    ts = tm // streams if streaming else tm
    nseq = tm // chunk
    row = lax.broadcasted_iota(jnp.int32, (tm, LANES), 0)
    pos = row & (chunk - 1)

    if streaming:
        @pl.when(pl.program_id(0) == 0)
        def _():
            ns_ref[...] = s0_ref[...]
            xn_scr[...] = _rms(x_ref[...].reshape(tm, D_MODEL), g1_ref[...]).astype(BF16)
            pad_a[...] = jnp.zeros(pad_a.shape, F32)
            pad_q[...] = jnp.zeros(pad_q.shape, F32)
            pad_a[:, SUBLANES - (CONV_A_WIDTH - 1):SUBLANES, :] = sa_ref[...]
            pad_q[:, SUBLANES - (CONV_QKV_WIDTH - 1):SUBLANES, :] = sq_ref[...]

    fillers = []

    def fill(n=1):
        for _ in range(min(n, len(fillers))):
            fillers.pop(0)()

    def queue_projection(lhs, w_ref, col0, width, out):
        def piece(j):
            def run():
                out[j] = _mm(lhs(), w_ref[:, col0 + j * PIECE:col0 + (j + 1) * PIECE])
            return run
        fillers.extend(piece(j) for j in range(width // PIECE))

    def causal_conv(pad, xin, w_ref, width, state_ref, new_ref, act):
        hw = width - 1
        taps = [w_ref[j:j + 1, :] for j in range(width)]
        if streaming:
            outs = []
            for b in range(streams):
                for r0 in range(b * ts, (b + 1) * ts, BLOCK_ROWS):
                    front = pad[b] if r0 == b * ts else xin[r0 - SUBLANES:r0]
                    slab = jnp.concatenate([front, xin[r0:r0 + BLOCK_ROWS]], axis=0)
                    y = xin[r0:r0 + BLOCK_ROWS] * taps[hw]
                    for s in range(1, width):
                        y = y + pltpu.roll(slab, s, 0)[SUBLANES:] * taps[hw - s]
                    outs.append(act(y))
                pad[b] = xin[(b + 1) * ts - SUBLANES:(b + 1) * ts]
                new_ref[b] = pad[b, SUBLANES - hw:SUBLANES, :]
            return jnp.concatenate(outs, axis=0)
        pad[0:tm, :] = xin
        pad[tm:2 * tm, :] = jnp.zeros((tm, xin.shape[1]), F32)
        for q in range(nseq):
            qp = (q - 1) % nseq
            pad[tm + qp * chunk + chunk - hw:tm + qp * chunk + chunk, :] = state_ref[q]
            new_ref[q] = pad[q * chunk + chunk - hw:q * chunk + chunk, :]
        hist = pad[tm:2 * tm, :]
        cpos = lax.broadcasted_iota(jnp.int32, (tm, 1), 0) & (chunk - 1)
        y = xin * taps[hw]
        for s in range(1, width):
            sh = jnp.where(cpos >= s, pltpu.roll(xin, s, 0), pltpu.roll(hist, s, 0))
            y = y + sh * taps[hw - s]
        return act(y)

    def wrap_lanes(piece, src):
        lane = lax.broadcasted_iota(jnp.int32, (tm, LANES), 1)
        return jnp.concatenate(
            [jnp.where(lane < GATE_SHIFT, src, piece[:, 0:LANES]), piece[:, LANES:]], axis=1)

    x = x_ref[...].reshape(tm, D_MODEL)
    xn = xn_scr[...] if streaming else _rms(x, g1_ref[...]).astype(BF16)

    pa_cols = [None] * (3 * W_A // PIECE)
    z_cols = [None] * (W_VV // PIECE)
    gate_cols = [None] * (2 * D_MODEL // PIECE)
    ya_cols = [None] * (D_MODEL // PIECE)
    queue_projection(lambda: xn, win_ref, OFF_A, 3 * W_A, pa_cols)
    queue_projection(lambda: xn, win_ref, OFF_Z, W_VV, z_cols)
    queue_projection(lambda: xn, win_ref, OFF_G, 2 * D_MODEL, gate_cols)

    p_q = _mm(xn, win_ref[:, OFF_QKV:OFF_QKV + W_QKV])
    small = _mm(xn, wsmall_ref[...])
    ba_raw = small[:, 0:LANES]
    gate_tail = small[:, LANES:]
    beta = _sigmoid(ba_raw)
    c = causal_conv(pad_q, p_q, cqw_ref, CONV_QKV_WIDTH, sq_ref, nq_ref, lambda y: y * _sigmoid(y))
    fill(len(pa_cols) + len(z_cols))

    g = -jnp.exp(alog_ref[...]) * _softplus(ba_raw + dtb_ref[...])
    gc = _chunk_scan(g, pos, chunk, reverse=False)
    g_after = _chunk_scan(g, pos, chunk, reverse=True) - g
    eg = jnp.exp(gc)
    e_after = jnp.exp(g_after)
    gc_t = gc.T

    ri = lax.broadcasted_iota(jnp.int32, (BLOCK_ROWS, BLOCK_ROWS), 0)
    ci = lax.broadcasted_iota(jnp.int32, (BLOCK_ROWS, BLOCK_ROWS), 1)
    m_incl = ((ri & -chunk) == (ci & -chunk)) & (ri >= ci)
    eye = jnp.where(ri == ci, 1.0, 0.0)
    m_levels = []
    size = 1
    while size < chunk:
        m_levels.append(((ri & -(2 * size)) == (ci & -(2 * size))) & ((ri & -size) > (ci & -size)))
        size *= 2

    pairs = [(rb, h) for rb in range(tm // BLOCK_ROWS) for h in range(N_HEADS)]
    q_g, k_t, rhs, p_blk, a_blk, t_inv = {}, {}, {}, {}, {}, {}
    for rb, h in pairs:
        rows = slice(rb * BLOCK_ROWS, (rb + 1) * BLOCK_ROWS)
        qh = c[rows, h * HEAD_K:(h + 1) * HEAD_K]
        kh = c[rows, W_QK + h * HEAD_K:W_QK + (h + 1) * HEAD_K]
        vh = c[rows, 2 * W_QK + h * HEAD_V:2 * W_QK + (h + 1) * HEAD_V]
        qh = qh * (lax.rsqrt(jnp.sum(qh * qh, axis=-1, keepdims=True) + EPS) * (HEAD_K ** -0.5))
        kh = kh * lax.rsqrt(jnp.sum(kh * kh, axis=-1, keepdims=True) + EPS)
        b_col = beta[rows, h:h + 1]
        hg = N_HEADS + h
        eg_col = eg[rows, hg:hg + 1]
        diff = gc[rows, hg:hg + 1] - gc_t[hg:hg + 1, rows]
        dec_incl = jnp.exp(jnp.where(m_incl, diff, -1e30))
        qk = _mm_nt(jnp.concatenate([qh, kh], axis=0), kh)
        p_blk[rb, h] = qk[0:BLOCK_ROWS] * dec_incl
        a_blk[rb, h] = qk[BLOCK_ROWS:] * dec_incl * b_col
        rhs[rb, h] = jnp.concatenate([vh * b_col, kh * (b_col * eg_col)], axis=1)
        q_g[rb, h] = qh * eg_col
        k_t[rb, h] = kh * e_after[rows, hg:hg + 1]
        t_inv[rb, h] = eye - jnp.where(m_levels[0], a_blk[rb, h], 0.0)

    n_gp = D_MODEL // PIECE
    fill(n_gp + 1)

    p_a = jnp.concatenate(pa_cols, axis=1)
    gated_u = causal_conv(pad_a, p_a[:, W_A:2 * W_A] * p_a[:, 2 * W_A:3 * W_A], caw_ref,
                          CONV_A_WIDTH, sa_ref, na_ref, lambda y: y) * p_a[:, 0:W_A]
    gated_u = gated_u.astype(BF16)
    queue_projection(lambda: gated_u, waout_ref, 0, D_MODEL, ya_cols)
    z = jnp.concatenate(z_cols, axis=1)
    z_act = z * _sigmoid(z)
    sig_a = _sigmoid(jnp.concatenate(
        [wrap_lanes(gate_cols[0], gate_cols[n_gp][:, 0:LANES])] + gate_cols[1:n_gp], axis=1))

    for m_off in m_levels[1:]:
        a_t = {p: _mm(jnp.where(m_off, a_blk[p], 0.0), t_inv[p]) for p in pairs}
        t_inv = {p: t_inv[p] - _mm(t_inv[p], a_t[p]) for p in pairs}
    w_vk = {p: _mm(t_inv[p], rhs[p]) for p in pairs}

    cpb = BLOCK_ROWS // chunk
    if streaming:
        cps = ts // chunk
        groups = [[(b * cps + j, h, b) for b in range(streams) for h in range(N_HEADS)]
                  for j in range(cps)]
    else:
        groups = [[(q, h, q) for q in range(nseq) for h in range(N_HEADS)]]
    for group in groups:
        s_old, xs, u_c = {}, {}, {}
        for q, h, slot in group:
            p, c0 = (q // cpb, h), (q % cpb) * chunk
            s_old[q, h] = ns_ref[slot, h] if streaming else s0_ref[slot, h]
            xs[q, h] = _mm(jnp.concatenate([w_vk[p][c0:c0 + chunk, HEAD_V:], q_g[p][c0:c0 + chunk]],
                                           axis=0), s_old[q, h])
        fill()
        for q, h, slot in group:
            p, c0 = (q // cpb, h), (q % cpb) * chunk
            u_c[q, h] = w_vk[p][c0:c0 + chunk, 0:HEAD_V] - xs[q, h][0:chunk]
            o_c = xs[q, h][chunk:] + _mm(p_blk[p][c0:c0 + chunk, c0:c0 + chunk], u_c[q, h])
            o_scr[q * chunk:(q + 1) * chunk, h * HEAD_V:(h + 1) * HEAD_V] = o_c
        for q, h, slot in group:
            p, c0 = (q // cpb, h), (q % cpb) * chunk
            g_last = eg[(q + 1) * chunk - 1:(q + 1) * chunk, N_HEADS + h:N_HEADS + h + 1]
            ns_ref[slot, h] = s_old[q, h] * g_last + _mm_tn(k_t[p][c0:c0 + chunk], u_c[q, h])
        fill()
    fill(len(fillers))

    og = og_ref[...]
    o_parts = []
    for h in range(N_HEADS):
        o_h = o_scr[:, h * HEAD_V:(h + 1) * HEAD_V]
        o_parts.append(_rms(o_h, og) * z_act[:, h * HEAD_V:(h + 1) * HEAD_V])
    y_b = _mm(jnp.concatenate(o_parts, axis=1), wbout_ref[...])

    sig_b = _sigmoid(jnp.concatenate(
        [wrap_lanes(gate_cols[n_gp], gate_tail)] + gate_cols[n_gp + 1:], axis=1))
    mixed = sig_a * jnp.concatenate(ya_cols, axis=1) + sig_b * y_b
    if streaming:
        xn_scr[...] = _rms(xnext_ref[...].reshape(tm, D_MODEL), g1_ref[...]).astype(BF16)
        wup_bf_ref[...] = wup_f32_ref[...].astype(BF16)
        wdown_bf_ref[...] = wdown_f32_ref[...].astype(BF16)
    h_ref[...] = (x + _mm(mixed, wo_ref[...])).reshape(h_ref.shape)


def _mlp_kernel(ha_ref, hb_ref, g2_ref, wup_ref, wdown_ref, gf_ref, outa_ref, outb_ref, *,
                final_norm, n_a):
    def body(h_ref, out_ref):
        h = h_ref[...]
        hn = _rms(h, g2_ref[...]).astype(BF16)
        acc = h
        for j in range(D_FF // FF_CHUNK):
            up = jnp.dot(hn, wup_ref[:, j * FF_CHUNK:(j + 1) * FF_CHUNK], preferred_element_type=F32)
            act = jnp.maximum(up, 0.0)
            acc = acc + _mm(act * act, wdown_ref[j * FF_CHUNK:(j + 1) * FF_CHUNK, :])
        if final_norm:
            acc = _rms(acc, gf_ref[...])
        out_ref[...] = acc

    step = pl.program_id(0)
    pl.when(step < n_a)(functools.partial(body, ha_ref, outa_ref))
    pl.when(step >= n_a)(functools.partial(body, hb_ref, outb_ref))


def _resident(shape, layer):
    nd = len(shape)
    return pl.BlockSpec((None,) + shape, lambda *_: (layer,) + (0,) * nd,
                        pipeline_mode=pl.Buffered(1))


def _mixer_weight_specs(layer):
    return [
        _resident((1, D_MODEL), layer),
        _resident((D_MODEL, IN_WIDTH), layer),
        _resident((D_MODEL, 2 * LANES), layer),
        _resident((CONV_A_WIDTH, W_A), layer),
        _resident((W_A, D_MODEL), layer),
        _resident((CONV_QKV_WIDTH, W_QKV), layer),
        _resident((1, LANES), layer),
        _resident((1, LANES), layer),
        _resident((1, HEAD_V), layer),
        _resident((W_VV, D_MODEL), layer),
        _resident((D_MODEL, D_MODEL), layer),
    ]


def _alias_kwargs(n_inputs, stacked_prev):
    specs = [pl.BlockSpec(memory_space=pl.ANY)] * len(stacked_prev)
    aliases = {n_inputs + k: 1 + k for k in range(len(stacked_prev))}
    return list(stacked_prev), specs, aliases


def _mixer_stream(x, buf_a, buf_qkv, s0, weights, layer, depth, stacked_prev, w_up, w_down):
    nb, seq, _ = x.shape
    tm = MIXER_TILE
    ts = tm // nb
    n_steps = seq // ts
    ff_blk = D_FF // n_steps
    whole3 = lambda i: (0, 0, 0)
    whole4 = lambda i: (0, 0, 0, 0)
    state_specs = [
        pl.BlockSpec((nb, CONV_A_WIDTH - 1, W_A), whole3),
        pl.BlockSpec((nb, CONV_QKV_WIDTH - 1, W_QKV), whole3),
        pl.BlockSpec((nb, N_HEADS, HEAD_K, HEAD_V), whole4),
    ]
    new_state_specs = [
        pl.BlockSpec((None, nb, CONV_A_WIDTH - 1, W_A), lambda i: (layer, 0, 0, 0)),
        pl.BlockSpec((None, nb, CONV_QKV_WIDTH - 1, W_QKV), lambda i: (layer, 0, 0, 0)),
        pl.BlockSpec((None, nb, N_HEADS, HEAD_K, HEAD_V), lambda i: (layer, 0, 0, 0, 0)),
    ]
    in_specs = ([pl.BlockSpec((nb, ts, D_MODEL), lambda i: (0, i, 0))]
                + _mixer_weight_specs(layer) + state_specs
                + [pl.BlockSpec((nb, ts, D_MODEL), lambda i: (0, jnp.minimum(i + 1, n_steps - 1), 0)),
                   pl.BlockSpec((None, D_MODEL, ff_blk), lambda i: (layer, 0, i)),
                   pl.BlockSpec((None, ff_blk, D_MODEL), lambda i: (layer, i, 0))])
    prev, prev_specs, aliases = _alias_kwargs(len(in_specs), stacked_prev)
    kern = functools.partial(_mixer_kernel, tm=tm, chunk=PROMPT_CHUNK, streams=nb, n_alias=len(prev))
    return pl.pallas_call(
        kern,
        grid=(n_steps,),
        in_specs=in_specs + prev_specs,
        out_specs=[pl.BlockSpec((nb, ts, D_MODEL), lambda i: (0, i, 0))] + new_state_specs + [
            pl.BlockSpec((None, D_MODEL, ff_blk), lambda i: (0, 0, i)),
            pl.BlockSpec((None, ff_blk, D_MODEL), lambda i: (0, i, 0)),
        ],
        out_shape=[
            jax.ShapeDtypeStruct(x.shape, F32),
            jax.ShapeDtypeStruct((depth,) + buf_a.shape, F32),
            jax.ShapeDtypeStruct((depth,) + buf_qkv.shape, F32),
            jax.ShapeDtypeStruct((depth,) + s0.shape, F32),
            jax.ShapeDtypeStruct((1, D_MODEL, D_FF), BF16),
            jax.ShapeDtypeStruct((1, D_FF, D_MODEL), BF16),
        ],
        input_output_aliases=aliases,
        scratch_shapes=[
            pltpu.VMEM((nb, SUBLANES, W_A), F32),
            pltpu.VMEM((nb, SUBLANES, W_QKV), F32),
            pltpu.VMEM((tm, W_VV), F32),
            pltpu.VMEM((tm, D_MODEL), BF16),
        ],
        compiler_params=pltpu.CompilerParams(
            dimension_semantics=("arbitrary",), vmem_limit_bytes=VMEM_LIMIT),
        name="mixer_stream",
    )(x, *weights, buf_a, buf_qkv, s0, x, w_up, w_down, *prev)


def _mixer_step(x, buf_a, buf_qkv, s0, weights, layer, stacked_prev):
    nb, seq, _ = x.shape
    tm = STEP_TILE
    nseq = tm // seq
    lay3 = lambda i: (layer, i, 0, 0)
    lay4 = lambda i: (layer, i, 0, 0, 0)
    x2 = x.reshape(nb * seq, D_MODEL)
    state_specs = [
        pl.BlockSpec((None, nseq, CONV_A_WIDTH - 1, W_A), lay3),
        pl.BlockSpec((None, nseq, CONV_QKV_WIDTH - 1, W_QKV), lay3),
        pl.BlockSpec((None, nseq, N_HEADS, HEAD_K, HEAD_V), lay4),
    ]
    in_specs = [pl.BlockSpec((tm, D_MODEL), lambda i: (i, 0))] + _mixer_weight_specs(layer) + state_specs
    prev, prev_specs, aliases = _alias_kwargs(len(in_specs), stacked_prev)
    kern = functools.partial(_mixer_kernel, tm=tm, chunk=seq, streams=0, n_alias=len(prev))
    h2, na, nq, ns = pl.pallas_call(
        kern,
        grid=(nb // nseq,),
        in_specs=in_specs + prev_specs,
        out_specs=[pl.BlockSpec((tm, D_MODEL), lambda i: (i, 0))] + state_specs,
        out_shape=[
            jax.ShapeDtypeStruct(x2.shape, F32),
            jax.ShapeDtypeStruct(buf_a.shape, F32),
            jax.ShapeDtypeStruct(buf_qkv.shape, F32),
            jax.ShapeDtypeStruct(s0.shape, F32),
        ],
        input_output_aliases=aliases,
        scratch_shapes=[
            pltpu.VMEM((2 * tm, W_A), F32),
            pltpu.VMEM((2 * tm, W_QKV), F32),
            pltpu.VMEM((tm, W_VV), F32),
        ],
        compiler_params=pltpu.CompilerParams(
            dimension_semantics=("arbitrary",), vmem_limit_bytes=VMEM_LIMIT),
        name="mixer_step",
    )(x2, *weights, buf_a, buf_qkv, s0, *prev)
    return h2.reshape(x.shape), na, nq, ns


def _mlp(h_a, h_b, g2, w_up, w_down, gf, layer, final_norm):
    a2 = h_a.reshape(-1, D_MODEL)
    b2 = h_b.reshape(-1, D_MODEL)
    ta = min(MLP_TILE, a2.shape[0])
    tb = min(MLP_TILE, b2.shape[0])
    n_a = a2.shape[0] // ta
    n_b = b2.shape[0] // tb
    a_idx = lambda i: (jnp.minimum(i, n_a - 1), 0)
    b_idx = lambda i: (jnp.maximum(i - n_a, 0), 0)
    out_a, out_b = pl.pallas_call(
        functools.partial(_mlp_kernel, final_norm=final_norm, n_a=n_a),
        grid=(n_a + n_b,),
        in_specs=[
            pl.BlockSpec((ta, D_MODEL), a_idx),
            pl.BlockSpec((tb, D_MODEL), b_idx),
            _resident((1, D_MODEL), layer),
            _resident((D_MODEL, D_FF), 0),
            _resident((D_FF, D_MODEL), 0),
            _resident((1, D_MODEL), 0),
        ],
        out_specs=[pl.BlockSpec((ta, D_MODEL), a_idx), pl.BlockSpec((tb, D_MODEL), b_idx)],
        out_shape=[jax.ShapeDtypeStruct(a2.shape, F32), jax.ShapeDtypeStruct(b2.shape, F32)],
        compiler_params=pltpu.CompilerParams(
            dimension_semantics=("arbitrary",), vmem_limit_bytes=VMEM_LIMIT),
        name="mlp",
    )(a2, b2, g2, w_up, w_down, gf)
    return out_a.reshape(h_a.shape), out_b.reshape(h_b.shape)


def _head_rows(v):
    return jnp.pad(v.astype(F32), ((0, 0), (N_HEADS, LANES - 2 * N_HEADS)))[:, None, :]


def kernel(x_prompt, x_sample, state_conv_a, state_conv_qkv, state_delta, norm1_g, w_in, conv_a_w,
           w_a_out, conv_qkv_w, a_log, dt_bias, onorm_g, w_b_out, w_o, norm2_g, w_up, w_down,
           final_g):
    depth = w_in.shape[0]
    nb = x_prompt.shape[0]
    zero_a = jnp.zeros((nb, CONV_A_WIDTH - 1, W_A), F32)
    zero_qkv = jnp.zeros((nb, CONV_QKV_WIDTH - 1, W_QKV), F32)
    zero_s = jnp.zeros((nb, N_HEADS, HEAD_K, HEAD_V), F32)

    w_in_b = w_in.astype(BF16)
    w_small = jnp.concatenate(
        [w_in_b[:, :, OFF_G:OFF_G + LANES],
         jnp.pad(w_in_b[:, :, IN_WIDTH - GATE_SHIFT:], ((0, 0), (0, 0), (0, LANES - GATE_SHIFT)))], axis=2)
    weights = (
        norm1_g[:, None, :], w_in_b, w_small, conv_a_w,
        jnp.roll(w_a_out.astype(BF16), GATE_SHIFT, axis=2), conv_qkv_w,
        _head_rows(a_log), _head_rows(dt_bias), onorm_g[:, None, :],
        jnp.roll(w_b_out.astype(BF16), GATE_SHIFT, axis=2),
        jnp.roll(w_o.astype(BF16), GATE_SHIFT, axis=1))
    g2 = norm2_g[:, None, :]
    gf = final_g.reshape(1, 1, D_MODEL)

    xp, xs = x_prompt, x_sample
    new_p = [jnp.zeros((depth,) + z.shape, F32) for z in (zero_a, zero_qkv, zero_s)]
    new_s = [jnp.zeros(s.shape, F32) for s in (state_conv_a, state_conv_qkv, state_delta)]
    for l in range(depth):
        last = l == depth - 1
        hp, *new_p, wu, wd = _mixer_stream(xp, zero_a, zero_qkv, zero_s, weights, l, depth, new_p,
                                           w_up, w_down)
        hs, *new_s = _mixer_step(xs, state_conv_a, state_conv_qkv, state_delta, weights, l, new_s)
        xp, xs = _mlp(hp, hs, g2, wu, wd, gf, l, last)

    return (xp, xs, *new_p, *new_s)
```

```python
import functools

import jax
import jax.numpy as jnp
from jax import lax
from jax.experimental import pallas as pl
from jax.experimental.pallas import tpu as pltpu

F32 = jnp.float32
BF16 = jnp.bfloat16

D_MODEL = 1024
W_A = 512
N_HEADS = 4
HEAD_K = 128
HEAD_V = 128
W_QK = N_HEADS * HEAD_K
W_VV = N_HEADS * HEAD_V
W_QKV = 2 * W_QK + W_VV
CONV_A_WIDTH = 3
CONV_QKV_WIDTH = 4
D_FF = 4 * D_MODEL
EPS = 1e-6

LANES = 128
SUBLANES = 8
BLOCK_ROWS = 128

OFF_A = 0
OFF_QKV = OFF_A + 3 * W_A
OFF_Z = OFF_QKV + W_QKV
OFF_G = OFF_Z + W_VV
GATE_SHIFT = 2 * N_HEADS
IN_ALIGNED = OFF_G + 2 * D_MODEL

PROMPT_CHUNK = 64
MIXER_TILE = 512
STEP_TILE = 256
MLP_TILE = 1024
FF_CHUNK = 1024
PIECE = 256
VMEM_LIMIT = 56 * 1024 * 1024


def _mm(a, b):
    return jnp.dot(a.astype(BF16), b.astype(BF16), preferred_element_type=F32)


def _mm_nt(a, b):
    return lax.dot_general(a.astype(BF16), b.astype(BF16), (((1,), (1,)), ((), ())),
                           preferred_element_type=F32)


def _mm_tn(a, b):
    return lax.dot_general(a.astype(BF16), b.astype(BF16), (((0,), (0,)), ((), ())),
                           preferred_element_type=F32)


def _rms(x, g):
    return x * lax.rsqrt(jnp.mean(x * x, axis=-1, keepdims=True) + EPS) * g


def _sigmoid(x):
    return 1.0 / (1.0 + jnp.exp(-x))


def _softplus(x):
    return jnp.maximum(x, 0.0) + jnp.log1p(jnp.exp(-jnp.abs(x)))


def _chunk_scan(x, pos, chunk, reverse):
    rows = x.shape[0]
    s = 1
    while s < chunk:
        if reverse:
            x = x + jnp.where(pos < chunk - s, pltpu.roll(x, rows - s, 0), 0.0)
        else:
            x = x + jnp.where(pos >= s, pltpu.roll(x, s, 0), 0.0)
        s *= 2
    return x


def _mixer_kernel(*refs, tm, chunk, streams, n_alias):
    streaming = streams > 0
    (x_ref, g1_ref, win_ref, wsmall_ref, caw_ref, waout_ref, cqw_ref, alog_ref, dtb_ref, og_ref,
     wbout_ref, wo_ref, sa_ref, sq_ref, s0_ref) = refs[:15]
    refs = refs[15:]
    if streaming:
        xnext_ref, wup_f32_ref, wdown_f32_ref = refs[:3]
        (h_ref, na_ref, nq_ref, ns_ref, wup_bf_ref, wdown_bf_ref,
         pad_a, pad_q, o_scr, xn_scr) = refs[3 + n_alias:]
    else:
        h_ref, na_ref, nq_ref, ns_ref, pad_a, pad_q, o_scr = refs[n_alias:]
    ts = tm // streams if streaming else tm
    nseq = tm // chunk
    row = lax.broadcasted_iota(jnp.int32, (tm, LANES), 0)
    pos = row & (chunk - 1)

    if streaming:
        @pl.when(pl.program_id(0) == 0)
        def _():
            ns_ref[...] = s0_ref[...]
            xn_scr[...] = _rms(x_ref[...].reshape(tm, D_MODEL), g1_ref[...]).astype(BF16)
            pad_a[...] = jnp.zeros(pad_a.shape, F32)
            pad_q[...] = jnp.zeros(pad_q.shape, F32)
            pad_a[:, SUBLANES - (CONV_A_WIDTH - 1):SUBLANES, :] = sa_ref[...]
            pad_q[:, SUBLANES - (CONV_QKV_WIDTH - 1):SUBLANES, :] = sq_ref[...]

    fillers = []

    def fill(n=1):
        for _ in range(min(n, len(fillers))):
            fillers.pop(0)()

    def queue_projection(lhs, w_ref, col0, width, out):
        def piece(j):
            def run():
                out[j] = _mm(lhs(), w_ref[:, col0 + j * PIECE:col0 + (j + 1) * PIECE])
            return run
        fillers.extend(piece(j) for j in range(width // PIECE))

    def causal_conv(pad, xin, w_ref, width, state_ref, new_ref, act):
        hw = width - 1
        taps = [w_ref[j:j + 1, :] for j in range(width)]
        if streaming:
            outs = []
            for b in range(streams):
                for r0 in range(b * ts, (b + 1) * ts, BLOCK_ROWS):
                    front = pad[b] if r0 == b * ts else xin[r0 - SUBLANES:r0]
                    slab = jnp.concatenate([front, xin[r0:r0 + BLOCK_ROWS]], axis=0)
                    y = xin[r0:r0 + BLOCK_ROWS] * taps[hw]
                    for s in range(1, width):
                        y = y + pltpu.roll(slab, s, 0)[SUBLANES:] * taps[hw - s]
                    outs.append(act(y))
                pad[b] = xin[(b + 1) * ts - SUBLANES:(b + 1) * ts]
                new_ref[b] = pad[b, SUBLANES - hw:SUBLANES, :]
            return jnp.concatenate(outs, axis=0)
        pad[0:tm, :] = xin
        pad[tm:2 * tm, :] = jnp.zeros((tm, xin.shape[1]), F32)
        for q in range(nseq):
            qp = (q - 1) % nseq
            pad[tm + qp * chunk + chunk - hw:tm + qp * chunk + chunk, :] = state_ref[q]
            new_ref[q] = pad[q * chunk + chunk - hw:q * chunk + chunk, :]
        hist = pad[tm:2 * tm, :]
        cpos = lax.broadcasted_iota(jnp.int32, (tm, 1), 0) & (chunk - 1)
        y = xin * taps[hw]
        for s in range(1, width):
            sh = jnp.where(cpos >= s, pltpu.roll(xin, s, 0), pltpu.roll(hist, s, 0))
            y = y + sh * taps[hw - s]
        return act(y)

    def wrap_lanes(piece, src):
        lane = lax.broadcasted_iota(jnp.int32, (tm, LANES), 1)
        return jnp.concatenate(
            [jnp.where(lane < GATE_SHIFT, src, piece[:, 0:LANES]), piece[:, LANES:]], axis=1)

    x = x_ref[...].reshape(tm, D_MODEL)
    xn = xn_scr[...] if streaming else _rms(x, g1_ref[...]).astype(BF16)

    pa_cols = [None] * (3 * W_A // PIECE)
    z_cols = [None] * (W_VV // PIECE)
    gate_cols = [None] * (2 * D_MODEL // PIECE)
    ya_cols = [None] * (D_MODEL // PIECE)
    queue_projection(lambda: xn, win_ref, OFF_A, 3 * W_A, pa_cols)
    queue_projection(lambda: xn, win_ref, OFF_Z, W_VV, z_cols)
    queue_projection(lambda: xn, win_ref, OFF_G, 2 * D_MODEL, gate_cols)

    p_q = _mm(xn, win_ref[:, OFF_QKV:OFF_QKV + W_QKV])
    small = _mm(xn, wsmall_ref[...])
    ba_raw = small[:, 0:LANES]
    gate_tail = small[:, LANES:]
    beta = _sigmoid(ba_raw)
    c = causal_conv(pad_q, p_q, cqw_ref, CONV_QKV_WIDTH, sq_ref, nq_ref, lambda y: y * _sigmoid(y))
    fill(len(pa_cols) + len(z_cols))

    g = -jnp.exp(alog_ref[...]) * _softplus(ba_raw + dtb_ref[...])
    gc = _chunk_scan(g, pos, chunk, reverse=False)
    g_after = _chunk_scan(g, pos, chunk, reverse=True) - g
    eg = jnp.exp(gc)
    e_after = jnp.exp(g_after)
    gc_t = gc.T

    ri = lax.broadcasted_iota(jnp.int32, (BLOCK_ROWS, BLOCK_ROWS), 0)
    ci = lax.broadcasted_iota(jnp.int32, (BLOCK_ROWS, BLOCK_ROWS), 1)
    m_incl = ((ri & -chunk) == (ci & -chunk)) & (ri >= ci)
    eye = jnp.where(ri == ci, 1.0, 0.0)
    m_levels = []
    size = 1
    while size < chunk:
        m_levels.append(((ri & -(2 * size)) == (ci & -(2 * size))) & ((ri & -size) > (ci & -size)))
        size *= 2

    pairs = [(rb, h) for rb in range(tm // BLOCK_ROWS) for h in range(N_HEADS)]
    q_g, k_t, rhs, p_blk, a_blk, t_inv = {}, {}, {}, {}, {}, {}
    for rb, h in pairs:
        rows = slice(rb * BLOCK_ROWS, (rb + 1) * BLOCK_ROWS)
        qh = c[rows, h * HEAD_K:(h + 1) * HEAD_K]
        kh = c[rows, W_QK + h * HEAD_K:W_QK + (h + 1) * HEAD_K]
        vh = c[rows, 2 * W_QK + h * HEAD_V:2 * W_QK + (h + 1) * HEAD_V]
        qh = qh * (lax.rsqrt(jnp.sum(qh * qh, axis=-1, keepdims=True) + EPS) * (HEAD_K ** -0.5))
        kh = kh * lax.rsqrt(jnp.sum(kh * kh, axis=-1, keepdims=True) + EPS)
        b_col = beta[rows, h:h + 1]
        hg = N_HEADS + h
        eg_col = eg[rows, hg:hg + 1]
        diff = gc[rows, hg:hg + 1] - gc_t[hg:hg + 1, rows]
        dec_incl = jnp.exp(jnp.where(m_incl, diff, -1e30))
        qk = _mm_nt(jnp.concatenate([qh, kh], axis=0), kh)
        p_blk[rb, h] = qk[0:BLOCK_ROWS] * dec_incl
        a_blk[rb, h] = qk[BLOCK_ROWS:] * dec_incl * b_col
        rhs[rb, h] = jnp.concatenate([vh * b_col, kh * (b_col * eg_col)], axis=1)
        q_g[rb, h] = qh * eg_col
        k_t[rb, h] = kh * e_after[rows, hg:hg + 1]
        t_inv[rb, h] = eye - jnp.where(m_levels[0], a_blk[rb, h], 0.0)

    n_gp = D_MODEL // PIECE
    fill(n_gp + 1)

    p_a = jnp.concatenate(pa_cols, axis=1)
    gated_u = causal_conv(pad_a, p_a[:, W_A:2 * W_A] * p_a[:, 2 * W_A:3 * W_A], caw_ref,
                          CONV_A_WIDTH, sa_ref, na_ref, lambda y: y) * p_a[:, 0:W_A]
    gated_u = gated_u.astype(BF16)
    queue_projection(lambda: gated_u, waout_ref, 0, D_MODEL, ya_cols)
    z = jnp.concatenate(z_cols, axis=1)
    z_act = z * _sigmoid(z)
    sig_a = _sigmoid(jnp.concatenate(
        [wrap_lanes(gate_cols[0], gate_cols[n_gp][:, 0:LANES])] + gate_cols[1:n_gp], axis=1))

    for m_off in m_levels[1:]:
        a_t = {p: _mm(jnp.where(m_off, a_blk[p], 0.0), t_inv[p]) for p in pairs}
        t_inv = {p: t_inv[p] - _mm(t_inv[p], a_t[p]) for p in pairs}
    w_vk = {p: _mm(t_inv[p], rhs[p]) for p in pairs}

    cpb = BLOCK_ROWS // chunk
    if streaming:
        cps = ts // chunk
        groups = [[(b * cps + j, h, b) for b in range(streams) for h in range(N_HEADS)]
                  for j in range(cps)]
    else:
        groups = [[(q, h, q) for q in range(nseq) for h in range(N_HEADS)]]
    for group in groups:
        s_old, xs, u_c = {}, {}, {}
        for q, h, slot in group:
            p, c0 = (q // cpb, h), (q % cpb) * chunk
            s_old[q, h] = ns_ref[slot, h] if streaming else s0_ref[slot, h]
            xs[q, h] = _mm(jnp.concatenate([w_vk[p][c0:c0 + chunk, HEAD_V:], q_g[p][c0:c0 + chunk]],
                                           axis=0), s_old[q, h])
        fill()
        for q, h, slot in group:
            p, c0 = (q // cpb, h), (q % cpb) * chunk
            u_c[q, h] = w_vk[p][c0:c0 + chunk, 0:HEAD_V] - xs[q, h][0:chunk]
            o_c = xs[q, h][chunk:] + _mm(p_blk[p][c0:c0 + chunk, c0:c0 + chunk], u_c[q, h])
            o_scr[q * chunk:(q + 1) * chunk, h * HEAD_V:(h + 1) * HEAD_V] = o_c
        for q, h, slot in group:
            p, c0 = (q // cpb, h), (q % cpb) * chunk
            g_last = eg[(q + 1) * chunk - 1:(q + 1) * chunk, N_HEADS + h:N_HEADS + h + 1]
            ns_ref[slot, h] = s_old[q, h] * g_last + _mm_tn(k_t[p][c0:c0 + chunk], u_c[q, h])
        fill()
    fill(len(fillers))

    og = og_ref[...]
    o_parts = []
    for h in range(N_HEADS):
        o_h = o_scr[:, h * HEAD_V:(h + 1) * HEAD_V]
        o_parts.append(_rms(o_h, og) * z_act[:, h * HEAD_V:(h + 1) * HEAD_V])
    y_b = _mm(jnp.concatenate(o_parts, axis=1), wbout_ref[...])

    sig_b = _sigmoid(jnp.concatenate(
        [wrap_lanes(gate_cols[n_gp], gate_tail)] + gate_cols[n_gp + 1:], axis=1))
    mixed = sig_a * jnp.concatenate(ya_cols, axis=1) + sig_b * y_b
    if streaming:
        xn_scr[...] = _rms(xnext_ref[...].reshape(tm, D_MODEL), g1_ref[...]).astype(BF16)
        wup_bf_ref[...] = wup_f32_ref[...].astype(BF16)
        wdown_bf_ref[...] = wdown_f32_ref[...].astype(BF16)
    h_ref[...] = (x + _mm(mixed, wo_ref[...])).reshape(h_ref.shape)


def _mlp_kernel(ha_ref, hb_ref, g2_ref, wup_ref, wdown_ref, gf_ref, outa_ref, outb_ref, *,
                final_norm, n_a):
    def body(h_ref, out_ref):
        h = h_ref[...]
        hn = _rms(h, g2_ref[...]).astype(BF16)
        acc = h
        for j in range(D_FF // FF_CHUNK):
            up = jnp.dot(hn, wup_ref[:, j * FF_CHUNK:(j + 1) * FF_CHUNK], preferred_element_type=F32)
            act = jnp.maximum(up, 0.0)
            acc = acc + _mm(act * act, wdown_ref[j * FF_CHUNK:(j + 1) * FF_CHUNK, :])
        if final_norm:
            acc = _rms(acc, gf_ref[...])
        out_ref[...] = acc

    step = pl.program_id(0)
    pl.when(step < n_a)(functools.partial(body, ha_ref, outa_ref))
    pl.when(step >= n_a)(functools.partial(body, hb_ref, outb_ref))


def _resident(shape, layer):
    nd = len(shape)
    return pl.BlockSpec((None,) + shape, lambda *_: (layer,) + (0,) * nd,
                        pipeline_mode=pl.Buffered(1))


def _mixer_weight_specs(layer):
    return [
        _resident((1, D_MODEL), layer),
        _resident((D_MODEL, IN_ALIGNED), layer),
        _resident((D_MODEL, 2 * LANES), layer),
        _resident((CONV_A_WIDTH, W_A), layer),
        _resident((W_A, D_MODEL), layer),
        _resident((CONV_QKV_WIDTH, W_QKV), layer),
        _resident((1, LANES), layer),
        _resident((1, LANES), layer),
        _resident((1, HEAD_V), layer),
        _resident((W_VV, D_MODEL), layer),
        _resident((D_MODEL, D_MODEL), layer),
    ]


def _alias_kwargs(n_inputs, stacked_prev):
    specs = [pl.BlockSpec(memory_space=pl.ANY)] * len(stacked_prev)
    aliases = {n_inputs + k: 1 + k for k in range(len(stacked_prev))}
    return list(stacked_prev), specs, aliases


def _mixer_stream(x, buf_a, buf_qkv, s0, weights, layer, depth, stacked_prev, w_up, w_down):
    nb, seq, _ = x.shape
    tm = MIXER_TILE
    ts = tm // nb
    n_steps = seq // ts
    ff_blk = D_FF // n_steps
    whole3 = lambda i: (0, 0, 0)
    whole4 = lambda i: (0, 0, 0, 0)
    state_specs = [
        pl.BlockSpec((nb, CONV_A_WIDTH - 1, W_A), whole3),
        pl.BlockSpec((nb, CONV_QKV_WIDTH - 1, W_QKV), whole3),
        pl.BlockSpec((nb, N_HEADS, HEAD_K, HEAD_V), whole4),
    ]
    new_state_specs = [
        pl.BlockSpec((None, nb, CONV_A_WIDTH - 1, W_A), lambda i: (layer, 0, 0, 0)),
        pl.BlockSpec((None, nb, CONV_QKV_WIDTH - 1, W_QKV), lambda i: (layer, 0, 0, 0)),
        pl.BlockSpec((None, nb, N_HEADS, HEAD_K, HEAD_V), lambda i: (layer, 0, 0, 0, 0)),
    ]
    in_specs = ([pl.BlockSpec((nb, ts, D_MODEL), lambda i: (0, i, 0))]
                + _mixer_weight_specs(layer) + state_specs
                + [pl.BlockSpec((nb, ts, D_MODEL), lambda i: (0, jnp.minimum(i + 1, n_steps - 1), 0)),
                   pl.BlockSpec((None, D_MODEL, ff_blk), lambda i: (layer, 0, i)),
                   pl.BlockSpec((None, ff_blk, D_MODEL), lambda i: (layer, i, 0))])
    prev, prev_specs, aliases = _alias_kwargs(len(in_specs), stacked_prev)
    kern = functools.partial(_mixer_kernel, tm=tm, chunk=PROMPT_CHUNK, streams=nb, n_alias=len(prev))
    return pl.pallas_call(
        kern,
        grid=(n_steps,),
        in_specs=in_specs + prev_specs,
        out_specs=[pl.BlockSpec((nb, ts, D_MODEL), lambda i: (0, i, 0))] + new_state_specs + [
            pl.BlockSpec((None, D_MODEL, ff_blk), lambda i: (0, 0, i)),
            pl.BlockSpec((None, ff_blk, D_MODEL), lambda i: (0, i, 0)),
        ],
        out_shape=[
            jax.ShapeDtypeStruct(x.shape, F32),
            jax.ShapeDtypeStruct((depth,) + buf_a.shape, F32),
            jax.ShapeDtypeStruct((depth,) + buf_qkv.shape, F32),
            jax.ShapeDtypeStruct((depth,) + s0.shape, F32),
            jax.ShapeDtypeStruct((1, D_MODEL, D_FF), BF16),
            jax.ShapeDtypeStruct((1, D_FF, D_MODEL), BF16),
        ],
        input_output_aliases=aliases,
        scratch_shapes=[
            pltpu.VMEM((nb, SUBLANES, W_A), F32),
            pltpu.VMEM((nb, SUBLANES, W_QKV), F32),
            pltpu.VMEM((tm, W_VV), F32),
            pltpu.VMEM((tm, D_MODEL), BF16),
        ],
        compiler_params=pltpu.CompilerParams(
            dimension_semantics=("arbitrary",), vmem_limit_bytes=VMEM_LIMIT),
        name="mixer_stream",
    )(x, *weights, buf_a, buf_qkv, s0, x, w_up, w_down, *prev)


def _mixer_step(x, buf_a, buf_qkv, s0, weights, layer, stacked_prev):
    nb, seq, _ = x.shape
    tm = STEP_TILE
    nseq = tm // seq
    lay3 = lambda i: (layer, i, 0, 0)
    lay4 = lambda i: (layer, i, 0, 0, 0)
    x2 = x.reshape(nb * seq, D_MODEL)
    state_specs = [
        pl.BlockSpec((None, nseq, CONV_A_WIDTH - 1, W_A), lay3),
        pl.BlockSpec((None, nseq, CONV_QKV_WIDTH - 1, W_QKV), lay3),
        pl.BlockSpec((None, nseq, N_HEADS, HEAD_K, HEAD_V), lay4),
    ]
    in_specs = [pl.BlockSpec((tm, D_MODEL), lambda i: (i, 0))] + _mixer_weight_specs(layer) + state_specs
    prev, prev_specs, aliases = _alias_kwargs(len(in_specs), stacked_prev)
    kern = functools.partial(_mixer_kernel, tm=tm, chunk=seq, streams=0, n_alias=len(prev))
    h2, na, nq, ns = pl.pallas_call(
        kern,
        grid=(nb // nseq,),
        in_specs=in_specs + prev_specs,
        out_specs=[pl.BlockSpec((tm, D_MODEL), lambda i: (i, 0))] + state_specs,
        out_shape=[
            jax.ShapeDtypeStruct(x2.shape, F32),
            jax.ShapeDtypeStruct(buf_a.shape, F32),
            jax.ShapeDtypeStruct(buf_qkv.shape, F32),
            jax.ShapeDtypeStruct(s0.shape, F32),
        ],
        input_output_aliases=aliases,
        scratch_shapes=[
            pltpu.VMEM((2 * tm, W_A), F32),
            pltpu.VMEM((2 * tm, W_QKV), F32),
            pltpu.VMEM((tm, W_VV), F32),
        ],
        compiler_params=pltpu.CompilerParams(
            dimension_semantics=("arbitrary",), vmem_limit_bytes=VMEM_LIMIT),
        name="mixer_step",
    )(x2, *weights, buf_a, buf_qkv, s0, *prev)
    return h2.reshape(x.shape), na, nq, ns


def _mlp(h_a, h_b, g2, w_up, w_down, gf, layer, final_norm):
    a2 = h_a.reshape(-1, D_MODEL)
    b2 = h_b.reshape(-1, D_MODEL)
    ta = min(MLP_TILE, a2.shape[0])
    tb = min(MLP_TILE, b2.shape[0])
    n_a = a2.shape[0] // ta
    n_b = b2.shape[0] // tb
    a_idx = lambda i: (jnp.minimum(i, n_a - 1), 0)
    b_idx = lambda i: (jnp.maximum(i - n_a, 0), 0)
    out_a, out_b = pl.pallas_call(
        functools.partial(_mlp_kernel, final_norm=final_norm, n_a=n_a),
        grid=(n_a + n_b,),
        in_specs=[
            pl.BlockSpec((ta, D_MODEL), a_idx),
            pl.BlockSpec((tb, D_MODEL), b_idx),
            _resident((1, D_MODEL), layer),
            _resident((D_MODEL, D_FF), 0),
            _resident((D_FF, D_MODEL), 0),
            _resident((1, D_MODEL), 0),
        ],
        out_specs=[pl.BlockSpec((ta, D_MODEL), a_idx), pl.BlockSpec((tb, D_MODEL), b_idx)],
        out_shape=[jax.ShapeDtypeStruct(a2.shape, F32), jax.ShapeDtypeStruct(b2.shape, F32)],
        compiler_params=pltpu.CompilerParams(
            dimension_semantics=("arbitrary",), vmem_limit_bytes=VMEM_LIMIT),
        name="mlp",
    )(a2, b2, g2, w_up, w_down, gf)
    return out_a.reshape(h_a.shape), out_b.reshape(h_b.shape)


def _head_rows(v):
    return jnp.pad(v.astype(F32), ((0, 0), (N_HEADS, LANES - 2 * N_HEADS)))[:, None, :]


def kernel(x_prompt, x_sample, state_conv_a, state_conv_qkv, state_delta, norm1_g, w_in, conv_a_w,
           w_a_out, conv_qkv_w, a_log, dt_bias, onorm_g, w_b_out, w_o, norm2_g, w_up, w_down,
           final_g):
    depth = w_in.shape[0]
    nb = x_prompt.shape[0]
    zero_a = jnp.zeros((nb, CONV_A_WIDTH - 1, W_A), F32)
    zero_qkv = jnp.zeros((nb, CONV_QKV_WIDTH - 1, W_QKV), F32)
    zero_s = jnp.zeros((nb, N_HEADS, HEAD_K, HEAD_V), F32)

    w_in_b = w_in[:, :, :IN_ALIGNED].astype(BF16)
    w_small = jnp.concatenate(
        [w_in[:, :, OFF_G:OFF_G + LANES],
         jnp.pad(w_in[:, :, IN_ALIGNED:], ((0, 0), (0, 0), (0, LANES - GATE_SHIFT)))], axis=2).astype(BF16)
    weights = (
        norm1_g[:, None, :], w_in_b, w_small, conv_a_w,
        jnp.roll(w_a_out.astype(BF16), GATE_SHIFT, axis=2), conv_qkv_w,
        _head_rows(a_log), _head_rows(dt_bias), onorm_g[:, None, :],
        jnp.roll(w_b_out.astype(BF16), GATE_SHIFT, axis=2),
        jnp.roll(w_o.astype(BF16), GATE_SHIFT, axis=1))
    g2 = norm2_g[:, None, :]
    gf = final_g.reshape(1, 1, D_MODEL)

    xp, xs = x_prompt, x_sample
    new_p = [jnp.zeros((depth,) + z.shape, F32) for z in (zero_a, zero_qkv, zero_s)]
    new_s = [jnp.zeros(s.shape, F32) for s in (state_conv_a, state_conv_qkv, state_delta)]
    for l in range(depth):
        last = l == depth - 1
        hp, *new_p, wu, wd = _mixer_stream(xp, zero_a, zero_qkv, zero_s, weights, l, depth, new_p,
                                           w_up, w_down)
        hs, *new_s = _mixer_step(xs, state_conv_a, state_conv_qkv, state_delta, weights, l, new_s)
        xp, xs = _mlp(hp, hs, g2, wu, wd, gf, l, last)

    return (xp, xs, *new_p, *new_s)
```
